```python
import math
import jax, jax.numpy as jnp
from jax import lax
import numpy as np

D_MODEL = 4096
BATCH = 2
SEQ = 4096
DEPTH = 1

RET_HEADS = 8
RET_HEAD_DIM = 256
RET_WIDTH = RET_HEADS * RET_HEAD_DIM
SG_GROUPS = 8
SG_GROUP_DIM = 256
SG_WIDTH = SG_GROUPS * SG_GROUP_DIM
CHUNK = 128
D_FF = -(-8 * D_MODEL // (3 * 256)) * 256
IN_WIDTHS = (RET_WIDTH, RET_WIDTH, RET_WIDTH, RET_WIDTH,
             SG_WIDTH, SG_WIDTH,
             D_MODEL, D_MODEL)
IN_WIDTH = sum(IN_WIDTHS)
ROPE_BASE = 10000.0
LN_EPS = 1e-5
ALPHA = (2.0 * DEPTH) ** 0.25
BETA = (8.0 * DEPTH) ** -0.25
ADA_SCALE = 0.1

kernel_name = "hybrid_retention_gmlp_deepnorm_adaln"


def _split_cols(z, widths):
    outs, start = [], 0
    for w in widths:
        outs.append(z[..., start:start + w])
        start += w
    return outs


def _layer_norm(x, g, b):
    xf = x.astype(jnp.float32)
    mu = jnp.mean(xf, axis=-1, keepdims=True)
    var = jnp.mean(jnp.square(xf - mu), axis=-1, keepdims=True)
    y = (xf - mu) * lax.rsqrt(var + LN_EPS)
    return (y * g.astype(jnp.float32) + b.astype(jnp.float32)).astype(x.dtype)


def _norm_noaffine(xf):
    mu = jnp.mean(xf, axis=-1, keepdims=True)
    var = jnp.mean(jnp.square(xf - mu), axis=-1, keepdims=True)
    return (xf - mu) * lax.rsqrt(var + LN_EPS)


def _rotary(x, pos):
    d = x.shape[-1]
    inv_freq = ROPE_BASE ** (-jnp.arange(0, d, 2, dtype=jnp.float32) / d)
    ang = pos.astype(jnp.float32)[..., None] * inv_freq
    cos = jnp.cos(ang)[:, :, None, :]
    sin = jnp.sin(ang)[:, :, None, :]
    x1, x2 = x[..., : d // 2], x[..., d // 2:]
    return jnp.concatenate([x1 * cos - x2 * sin, x2 * cos + x1 * sin], axis=-1)


def _retention_chunkwise(q, k, v):
    B, S, H, d = q.shape
    n_chunks = S // CHUNK
    log_g = jnp.log(1.0 - 2.0 ** (-5.0 - jnp.arange(H, dtype=jnp.float32)))
    idx = jnp.arange(CHUNK, dtype=jnp.float32)
    diff = idx[:, None] - idx[None, :]
    causal = diff >= 0
    inner_decay = jnp.where(causal[None], jnp.exp(log_g[:, None, None] * jnp.maximum(diff, 0.0)[None]), 0.0)
    xi = jnp.exp(log_g[:, None] * (idx[None, :] + 1.0))
    zeta = jnp.exp(log_g[:, None] * (CHUNK - 1.0 - idx[None, :]))
    chunk_decay = jnp.exp(log_g * CHUNK)

    def to_chunks(t):
        return t.reshape(B, n_chunks, CHUNK, H, d).transpose(1, 0, 3, 2, 4)

    def step(R, inp):
        qc, kc, vc = inp
        scores = jnp.einsum('bhnd,bhmd->bhnm', qc, kc) * inner_decay[None]
        inner = jnp.einsum('bhnm,bhme->bhne', scores, vc)
        cross = jnp.einsum('bhnd,bhde->bhne', qc, R) * xi[None, :, :, None]
        R_new = chunk_decay[None, :, None, None] * R + jnp.einsum(
            'bhmd,bhme->bhde', kc * zeta[None, :, :, None], vc)
        return R_new, inner + cross

    R0 = jnp.zeros((B, H, d, v.shape[-1]), jnp.float32)
    _, ys = lax.scan(step, R0, (to_chunks(q), to_chunks(k), to_chunks(v)))
    return ys.transpose(1, 0, 3, 2, 4).reshape(B, S, H, v.shape[-1])


def _spatial_gating(u, s, ln_g, ln_b, w_s, b_s):
    B, S, _ = s.shape
    n_chunks = S // CHUNK
    s = _layer_norm(s, ln_g, ln_b)
    mask = jnp.tril(jnp.ones((CHUNK, CHUNK), dtype=bool))
    w = jnp.where(mask[None], w_s, jnp.zeros_like(w_s))
    sc = s.reshape(B, n_chunks, CHUNK, SG_GROUPS, SG_GROUP_DIM)
    mixed = jnp.einsum('gts,bnsgc->bntgc', w, sc) + b_s.T[None, None, :, :, None]
    return u * mixed.reshape(B, S, SG_WIDTH)


def setup_inputs(seed: int = 0) -> dict:
    key = jax.random.key(seed)
    ks = jax.random.split(key, 24)
    f32 = jnp.float32
    L, D = DEPTH, D_MODEL

    def nrm(k, shape, scale):
        return jax.random.normal(k, shape, f32) * scale

    x = jax.random.normal(ks[0], (BATCH, SEQ, D), f32)
    c = jax.random.normal(ks[1], (BATCH, D), f32)
    offset = jax.random.randint(ks[2], (BATCH,), 0, SEQ, dtype=jnp.int32)
    positions = (offset[:, None] + jnp.arange(SEQ, dtype=jnp.int32)[None, :]).astype(jnp.int32)
    return {
        "x": x,
        "c": c,
        "positions": positions,
        "w_ada": nrm(ks[3], (L, D, 6 * D), ADA_SCALE * D ** -0.5),
        "b_ada": nrm(ks[4], (L, 6 * D), 0.01),
        "w_in": nrm(ks[5], (L, D, IN_WIDTH), D ** -0.5),
        "sg_ln_g": 1.0 + nrm(ks[6], (L, SG_WIDTH), 0.01),
        "sg_ln_b": nrm(ks[7], (L, SG_WIDTH), 0.01),
        "sg_w": nrm(ks[8], (L, SG_GROUPS, CHUNK, CHUNK), CHUNK ** -0.5),
        "sg_b": 1.0 + nrm(ks[9], (L, SG_GROUPS, CHUNK), 0.01),
        "w_ret_up": nrm(ks[10], (L, RET_WIDTH, D), RET_WIDTH ** -0.5),
        "w_sg_up": nrm(ks[11], (L, SG_WIDTH, D), SG_WIDTH ** -0.5),
        "w_out": nrm(ks[12], (L, D, D), BETA * D ** -0.5),
        "ln1_g": 1.0 + nrm(ks[13], (L, D), 0.01),
        "ln1_b": nrm(ks[14], (L, D), 0.01),
        "w_gate_up": nrm(ks[15], (L, D, 2 * D_FF), D ** -0.5),
        "w_down": nrm(ks[16], (L, D_FF, D), BETA * D_FF ** -0.5),
        "ln2_g": 1.0 + nrm(ks[17], (L, D), 0.01),
        "ln2_b": nrm(ks[18], (L, D), 0.01),
    }


def reference(x, c, positions, w_ada, b_ada, w_in, sg_ln_g, sg_ln_b, sg_w, sg_b,
              w_ret_up, w_sg_up, w_out, ln1_g, ln1_b, w_gate_up, w_down, ln2_g, ln2_b):
    B, S, D = x.shape
    for l in range(DEPTH):
        ada = jax.nn.silu(c) @ w_ada[l] + b_ada[l]
        sh1, sc1, g1, sh2, sc2, g2 = [t[:, None, :] for t in jnp.split(ada, 6, axis=-1)]

        h = x * (1.0 + sc1) + sh1
        z = h @ w_in[l]
        q, k, v, g, u, s, gate_ret, gate_sg = _split_cols(z, IN_WIDTHS)

        qh = _rotary(q.reshape(B, S, RET_HEADS, RET_HEAD_DIM).astype(jnp.float32), positions)
        kh = _rotary(k.reshape(B, S, RET_HEADS, RET_HEAD_DIM).astype(jnp.float32), positions) * (RET_HEAD_DIM ** -0.5)
        vh = v.reshape(B, S, RET_HEADS, RET_HEAD_DIM).astype(jnp.float32)
        ret = _norm_noaffine(_retention_chunkwise(qh, kh, vh))
        ret = ret.reshape(B, S, RET_WIDTH).astype(x.dtype) * jax.nn.silu(g)
        y_ret = ret @ w_ret_up[l]

        sgu = _spatial_gating(jax.nn.gelu(u), jax.nn.gelu(s), sg_ln_g[l], sg_ln_b[l], sg_w[l], sg_b[l])
        y_sg = sgu @ w_sg_up[l]

        mix = jax.nn.sigmoid(gate_ret) * y_ret + jax.nn.sigmoid(gate_sg) * y_sg
        x = _layer_norm(ALPHA * x + (1.0 + g1) * (mix @ w_out[l]), ln1_g[l], ln1_b[l])

        h = x * (1.0 + sc2) + sh2
        a = h @ w_gate_up[l]
        ffn = (jax.nn.silu(a[..., :D_FF]) * a[..., D_FF:]) @ w_down[l]
        x = _layer_norm(ALPHA * x + (1.0 + g2) * ffn, ln2_g[l], ln2_b[l])
    return x
```

```python
import functools

import jax
import jax.numpy as jnp
from jax import lax
from jax.experimental import pallas as pl
from jax.experimental.pallas import tpu as pltpu

F32 = jnp.float32
BF16 = jnp.bfloat16

RET_HEADS = 8
HEAD_DIM = 256
SG_GROUPS = 8
SG_GROUP_DIM = 256
CHUNK = 128
ROPE_BASE = 10000.0
LN_EPS = 1e-5
DEPTH = 1
ALPHA = (2.0 * DEPTH) ** 0.25

MIB = 1024 * 1024
V7X_VMEM_BUDGET = 56 * MIB


def _params(sem, vmem_bytes):
    return pltpu.CompilerParams(dimension_semantics=sem, vmem_limit_bytes=vmem_bytes)


ADA_TN = 512


def _ada_kernel(c_ref, w_ref, b_ref, o_ref):
    s = jax.nn.silu(c_ref[...]).astype(BF16)
    o_ref[...] = jnp.dot(s, w_ref[...].astype(BF16), preferred_element_type=F32) + b_ref[...]


def _ada(c_pad, w_ada, b_ada):
    rows, d = c_pad.shape
    n = w_ada.shape[1]
    return pl.pallas_call(
        _ada_kernel,
        grid=(n // ADA_TN,),
        in_specs=[pl.BlockSpec((rows, d), lambda j: (0, 0)),
                  pl.BlockSpec((d, ADA_TN), lambda j: (0, j)),
                  pl.BlockSpec((1, ADA_TN), lambda j: (0, j))],
        out_specs=pl.BlockSpec((rows, ADA_TN), lambda j: (0, j)),
        out_shape=jax.ShapeDtypeStruct((rows, n), F32),
        compiler_params=_params(("arbitrary",), 32 * MIB),
        name="ada",
    )(c_pad, w_ada, b_ada)


MOD_TR = 512


def _modulate_kernel(x_ref, sc_ref, sh_ref, o_ref):
    o_ref[...] = (x_ref[...] * (1.0 + sc_ref[0]) + sh_ref[0]).astype(BF16)


def _modulate(x2, sc, sh, seq):
    m, d = x2.shape
    per_batch = seq // MOD_TR
    vec = pl.BlockSpec((1, 1, d), lambda i: (i // per_batch, 0, 0))
    return pl.pallas_call(
        _modulate_kernel,
        grid=(m // MOD_TR,),
        in_specs=[pl.BlockSpec((MOD_TR, d), lambda i: (i, 0)), vec, vec],
        out_specs=pl.BlockSpec((MOD_TR, d), lambda i: (i, 0)),
        out_shape=jax.ShapeDtypeStruct((m, d), BF16),
        compiler_params=_params(("arbitrary",), 40 * MIB),
        name="modulate",
    )(x2, sc, sh)


ROPE_TR = 1024


def _rope_kernel(pos_ref, inv_ref, cos_ref, sin_ref):
    ang = pos_ref[...] * inv_ref[...]
    cos_ref[...] = jnp.cos(ang)
    sin_ref[...] = jnp.sin(ang)


def _rope_tables(pos_col, inv_freq):
    m = pos_col.shape[0]
    half = inv_freq.shape[1]
    out = jax.ShapeDtypeStruct((m, half), F32)
    return pl.pallas_call(
        _rope_kernel,
        grid=(m // ROPE_TR,),
        in_specs=[pl.BlockSpec((ROPE_TR, 1), lambda i: (i, 0)),
                  pl.BlockSpec((1, half), lambda i: (0, 0))],
        out_specs=[pl.BlockSpec((ROPE_TR, half), lambda i: (i, 0))] * 2,
        out_shape=[out, out],
        compiler_params=_params(("arbitrary",), 32 * MIB),
        name="rope",
    )(pos_col, inv_freq)


Z_TM = 1024
Z_TN = 512
Z_RC = 256


def _zproj_kernel(h_ref, w_ref, cos_ref, sin_ref, o_ref, *, bounds):
    n = pl.program_id(1)
    half = HEAD_DIM // 2

    def run(epilogue):
        for r in range(Z_TM // Z_RC):
            rows = slice(r * Z_RC, (r + 1) * Z_RC)
            acc = jnp.dot(h_ref[rows, :], w_ref[...], preferred_element_type=F32)
            epilogue(acc, rows)

    def rotary(scale):
        def epi(acc, rows):
            c = cos_ref[rows, :]
            s = sin_ref[rows, :]
            for j in range(Z_TN // HEAD_DIM):
                x1 = acc[:, j * HEAD_DIM: j * HEAD_DIM + half]
                x2 = acc[:, j * HEAD_DIM + half: (j + 1) * HEAD_DIM]
                o_ref[rows, j * HEAD_DIM: j * HEAD_DIM + half] = ((x1 * c - x2 * s) * scale).astype(BF16)
                o_ref[rows, j * HEAD_DIM + half: (j + 1) * HEAD_DIM] = ((x2 * c + x1 * s) * scale).astype(BF16)
        return epi

    def pointwise(fn):
        def epi(acc, rows):
            o_ref[rows, :] = fn(acc).astype(BF16)
        return epi

    q_end, k_end, v_end, g_end, us_end = bounds
    pl.when(n < q_end)(lambda: run(rotary(1.0)))
    pl.when((n >= q_end) & (n < k_end))(lambda: run(rotary(HEAD_DIM ** -0.5)))
    pl.when((n >= k_end) & (n < v_end))(lambda: run(pointwise(lambda a: a)))
    pl.when((n >= v_end) & (n < g_end))(lambda: run(pointwise(jax.nn.silu)))
    pl.when((n >= g_end) & (n < us_end))(lambda: run(pointwise(jax.nn.gelu)))
    pl.when(n >= us_end)(lambda: run(pointwise(jax.nn.sigmoid)))


def _zproj(h1, w_in, cos, sin):
    m, d = h1.shape
    n = w_in.shape[1]
    ret_w = RET_HEADS * HEAD_DIM
    sg_w = SG_GROUPS * SG_GROUP_DIM
    bounds = tuple(b // Z_TN for b in (ret_w, 2 * ret_w, 3 * ret_w, 4 * ret_w, 4 * ret_w + 2 * sg_w))
    half = cos.shape[1]
    return pl.pallas_call(
        functools.partial(_zproj_kernel, bounds=bounds),
        grid=(m // Z_TM, n // Z_TN),
        in_specs=[pl.BlockSpec((Z_TM, d), lambda i, j: (i, 0)),
                  pl.BlockSpec((d, Z_TN), lambda i, j: (0, j)),
                  pl.BlockSpec((Z_TM, half), lambda i, j: (i, 0)),
                  pl.BlockSpec((Z_TM, half), lambda i, j: (i, 0))],
        out_specs=pl.BlockSpec((Z_TM, Z_TN), lambda i, j: (i, j)),
        out_shape=jax.ShapeDtypeStruct((m, n), BF16),
        compiler_params=_params(("arbitrary", "arbitrary"), 48 * MIB),
        name="zproj",
    )(h1, w_in, cos, sin)


def _retention_kernel(cd_ref, q_ref, k_ref, v_ref, g_ref, dec_ref, xi_ref, zeta_ref, o_ref, state_ref):
    @pl.when(pl.program_id(1) == 0)
    def _():
        state_ref[...] = jnp.zeros_like(state_ref)

    for h in range(RET_HEADS):
        cols = slice(h * HEAD_DIM, (h + 1) * HEAD_DIM)
        q = q_ref[:, cols]
        k = k_ref[:, cols]
        v = v_ref[:, cols]
        scores = lax.dot_general(q, k, (((1,), (1,)), ((), ())), preferred_element_type=F32)
        scores = scores * dec_ref[h]
        inner = jnp.dot(scores.astype(BF16), v, preferred_element_type=F32)
        state = state_ref[h]
        cross = jnp.dot(q, state.astype(BF16), preferred_element_type=F32) * xi_ref[h]
        kz = (k.astype(F32) * zeta_ref[h]).astype(BF16)
        update = lax.dot_general(kz, v, (((0,), (0,)), ((), ())), preferred_element_type=F32)
        state_ref[h] = cd_ref[h] * state + update
        y = inner + cross
        mu = jnp.mean(y, axis=-1, keepdims=True)
        yc = y - mu
        var = jnp.mean(yc * yc, axis=-1, keepdims=True)
        yn = yc * lax.rsqrt(var + LN_EPS)
        o_ref[:, cols] = (yn * g_ref[:, cols].astype(F32)).astype(BF16)


def _retention(z, chunk_decay, inner_decay, xi_b, zeta_b, batch, seq):
    m = z.shape[0]
    width = RET_HEADS * HEAD_DIM
    n_chunks = seq // CHUNK

    def zblock(col):
        return pl.BlockSpec((CHUNK, width), lambda b, c: (b * n_chunks + c, col))

    def const(shape):
        return pl.BlockSpec(shape, lambda b, c: (0,) * len(shape))

    return pl.pallas_call(
        _retention_kernel,
        grid=(batch, n_chunks),
        in_specs=[pl.BlockSpec(memory_space=pltpu.SMEM),
                  zblock(0), zblock(1), zblock(2), zblock(3),
                  const(inner_decay.shape), const(xi_b.shape), const(zeta_b.shape)],
        out_specs=pl.BlockSpec((CHUNK, width), lambda b, c: (b * n_chunks + c, 0)),
        out_shape=jax.ShapeDtypeStruct((m, width), BF16),
        scratch_shapes=[pltpu.VMEM((RET_HEADS, HEAD_DIM, HEAD_DIM), F32)],
        compiler_params=_params(("arbitrary", "arbitrary"), 32 * MIB),
        name="retention",
    )(chunk_decay, z, z, z, z, inner_decay, xi_b, zeta_b)


def _sgu_kernel(u_ref, s_ref, lng_ref, lnb_ref, w_ref, b_ref, o_ref):
    s = s_ref[...].astype(F32)
    mu = jnp.mean(s, axis=-1, keepdims=True)
    sc = s - mu
    var = jnp.mean(sc * sc, axis=-1, keepdims=True)
    sn = (sc * lax.rsqrt(var + LN_EPS) * lng_ref[...] + lnb_ref[...]).astype(BF16)
    t_idx = lax.broadcasted_iota(jnp.int32, (CHUNK, CHUNK), 0)
    s_idx = lax.broadcasted_iota(jnp.int32, (CHUNK, CHUNK), 1)
    causal = t_idx >= s_idx
    for g in range(SG_GROUPS):
        cols = slice(g * SG_GROUP_DIM, (g + 1) * SG_GROUP_DIM)
        w = jnp.where(causal, w_ref[g], 0.0).astype(BF16)
        mixed = jnp.dot(w, sn[:, cols], preferred_element_type=F32) + b_ref[g]
        o_ref[:, cols] = (u_ref[:, cols].astype(F32) * mixed).astype(BF16)


def _sgu(z, ln_g, ln_b, w_s, b_col):
    m = z.shape[0]
    width = SG_GROUPS * SG_GROUP_DIM
    return pl.pallas_call(
        _sgu_kernel,
        grid=(m // CHUNK,),
        in_specs=[pl.BlockSpec((CHUNK, width), lambda i: (i, 4)),
                  pl.BlockSpec((CHUNK, width), lambda i: (i, 5)),
                  pl.BlockSpec((1, width), lambda i: (0, 0)),
                  pl.BlockSpec((1, width), lambda i: (0, 0)),
                  pl.BlockSpec(w_s.shape, lambda i: (0, 0, 0)),
                  pl.BlockSpec(b_col.shape, lambda i: (0, 0, 0))],
        out_specs=pl.BlockSpec((CHUNK, width), lambda i: (i, 0)),
        out_shape=jax.ShapeDtypeStruct((m, width), BF16),
        compiler_params=_params(("arbitrary",), 32 * MIB),
        name="sgu",
    )(z, z, ln_g, ln_b, w_s, b_col)


MIX_TM = 1024
MIX_TN = 512
MIX_RC = 256


def _mix_kernel(ret_ref, sgu_ref, wr_ref, ws_ref, gr_ref, gs_ref, o_ref):
    for r in range(MIX_TM // MIX_RC):
        rows = slice(r * MIX_RC, (r + 1) * MIX_RC)
        y_ret = jnp.dot(ret_ref[rows, :], wr_ref[...], preferred_element_type=F32)
        y_sg = jnp.dot(sgu_ref[rows, :], ws_ref[...], preferred_element_type=F32)
        mix = gr_ref[rows, :].astype(F32) * y_ret + gs_ref[rows, :].astype(F32) * y_sg
        o_ref[rows, :] = mix.astype(BF16)


def _mix(ret, sgu, w_ret_up, w_sg_up, z, gate_col0):
    m, kdim = ret.shape
    n = w_ret_up.shape[1]
    g0 = gate_col0 // MIX_TN
    g1 = g0 + n // MIX_TN
    return pl.pallas_call(
        _mix_kernel,
        grid=(m // MIX_TM, n // MIX_TN),
        in_specs=[pl.BlockSpec((MIX_TM, kdim), lambda i, j: (i, 0)),
                  pl.BlockSpec((MIX_TM, kdim), lambda i, j: (i, 0)),
                  pl.BlockSpec((kdim, MIX_TN), lambda i, j: (0, j)),
                  pl.BlockSpec((kdim, MIX_TN), lambda i, j: (0, j)),
                  pl.BlockSpec((MIX_TM, MIX_TN), lambda i, j: (i, g0 + j)),
                  pl.BlockSpec((MIX_TM, MIX_TN), lambda i, j: (i, g1 + j))],
        out_specs=pl.BlockSpec((MIX_TM, MIX_TN), lambda i, j: (i, j)),
        out_shape=jax.ShapeDtypeStruct((m, n), BF16),
        compiler_params=_params(("arbitrary", "arbitrary"), 40 * MIB),
        name="mix",
    )(ret, sgu, w_ret_up, w_sg_up, z, z)


OUT_TM = 1024
OUT_TN = 1024
OUT_RC = 256


def _outproj_kernel(a_ref, w_ref, o_ref):
    for r in range(OUT_TM // OUT_RC):
        rows = slice(r * OUT_RC, (r + 1) * OUT_RC)
        o_ref[rows, :] = jnp.dot(a_ref[rows, :], w_ref[...], preferred_element_type=F32)


def _outproj(a, w):
    m, kdim = a.shape
    n = w.shape[1]
    return pl.pallas_call(
        _outproj_kernel,
        grid=(m // OUT_TM, n // OUT_TN),
        in_specs=[pl.BlockSpec((OUT_TM, kdim), lambda i, j: (i, 0)),
                  pl.BlockSpec((kdim, OUT_TN), lambda i, j: (0, j))],
        out_specs=pl.BlockSpec((OUT_TM, OUT_TN), lambda i, j: (i, j)),
        out_shape=jax.ShapeDtypeStruct((m, n), F32),
        compiler_params=_params(("arbitrary", "arbitrary"), 48 * MIB),
        name="outproj",
    )(a, w)


LN_TR = 256


def _residual_ln(x, branch, gate, ln_g, ln_b):
    y = ALPHA * x + (1.0 + gate) * branch
    mu = jnp.mean(y, axis=-1, keepdims=True)
    yc = y - mu
    var = jnp.mean(yc * yc, axis=-1, keepdims=True)
    return yc * lax.rsqrt(var + LN_EPS) * ln_g + ln_b


def _ln1_kernel(x_ref, br_ref, gate_ref, lng_ref, lnb_ref, sc_ref, sh_ref, x1_ref, h2_ref):
    x1 = _residual_ln(x_ref[...], br_ref[...], gate_ref[0], lng_ref[...], lnb_ref[...])
    x1_ref[...] = x1
    h2_ref[...] = (x1 * (1.0 + sc_ref[0]) + sh_ref[0]).astype(BF16)


def _ln2_kernel(x_ref, br_ref, gate_ref, lng_ref, lnb_ref, o_ref):
    o_ref[...] = _residual_ln(x_ref[...], br_ref[...], gate_ref[0], lng_ref[...], lnb_ref[...])


def _ln_call(kernel, name, x, branch, batch_vecs, row_vecs, out_dtypes, seq):
    m, d = x.shape
    per_batch = seq // LN_TR
    tile = pl.BlockSpec((LN_TR, d), lambda i: (i, 0))
    bvec = pl.BlockSpec((1, 1, d), lambda i: (i // per_batch, 0, 0))
    rvec = pl.BlockSpec((1, d), lambda i: (0, 0))
    operands = [x, branch, batch_vecs[0], *row_vecs, *batch_vecs[1:]]
    specs = [tile, tile, bvec] + [rvec] * len(row_vecs) + [bvec] * (len(batch_vecs) - 1)
    return pl.pallas_call(
        kernel,
        grid=(m // LN_TR,),
        in_specs=specs,
        out_specs=[tile] * len(out_dtypes),
        out_shape=[jax.ShapeDtypeStruct((m, d), dt) for dt in out_dtypes],
        compiler_params=_params(("arbitrary",), 48 * MIB),
        name=name,
    )(*operands)


FFN_TM = 512
FFN_TF = 256
FFN_RC = 256


def _ffn_kernel(h_ref, wg_ref, wu_ref, wd_ref, o_ref):
    f = pl.program_id(1)

    def step(first):
        for r in range(FFN_TM // FFN_RC):
            rows = slice(r * FFN_RC, (r + 1) * FFN_RC)
            h = h_ref[rows, :]
            a_gate = jnp.dot(h, wg_ref[...], preferred_element_type=F32)
            a_up = jnp.dot(h, wu_ref[...], preferred_element_type=F32)
            act = (jax.nn.silu(a_gate) * a_up).astype(BF16)
            contrib = jnp.dot(act, wd_ref[...], preferred_element_type=F32)
            if first:
                o_ref[rows, :] = contrib
            else:
                o_ref[rows, :] += contrib

    pl.when(f == 0)(lambda: step(True))
    pl.when(f != 0)(lambda: step(False))


def _ffn(h2, w_gate_up, w_down):
    m, d = h2.shape
    d_ff = w_down.shape[0]
    nf = d_ff // FFN_TF
    return pl.pallas_call(
        _ffn_kernel,
        grid=(m // FFN_TM, nf),
        in_specs=[pl.BlockSpec((FFN_TM, d), lambda i, f: (i, 0)),
                  pl.BlockSpec((d, FFN_TF), lambda i, f: (0, f)),
                  pl.BlockSpec((d, FFN_TF), lambda i, f: (0, nf + f)),
                  pl.BlockSpec((FFN_TF, d), lambda i, f: (f, 0))],
        out_specs=pl.BlockSpec((FFN_TM, d), lambda i, f: (i, 0)),
        out_shape=jax.ShapeDtypeStruct((m, d), F32),
        compiler_params=_params(("arbitrary", "arbitrary"), 48 * MIB),
        name="ffn",
    )(h2, w_gate_up, w_gate_up, w_down)


def _decay_tables():
    log_g = jnp.log(1.0 - 2.0 ** (-5.0 - jnp.arange(RET_HEADS, dtype=F32)))
    idx = jnp.arange(CHUNK, dtype=F32)
    diff = idx[:, None] - idx[None, :]
    inner = jnp.where((diff >= 0)[None],
                      jnp.exp(log_g[:, None, None] * jnp.maximum(diff, 0.0)[None]), 0.0)
    xi = jnp.exp(log_g[:, None] * (idx[None, :] + 1.0))
    zeta = jnp.exp(log_g[:, None] * (CHUNK - 1.0 - idx[None, :]))
    chunk_decay = jnp.exp(log_g * CHUNK)
    xi_b = jnp.broadcast_to(xi[:, :, None], (RET_HEADS, CHUNK, HEAD_DIM))
    zeta_b = jnp.broadcast_to(zeta[:, :, None], (RET_HEADS, CHUNK, HEAD_DIM))
    return chunk_decay, inner, xi_b, zeta_b


def kernel(x, c, positions, w_ada, b_ada, w_in, sg_ln_g, sg_ln_b, sg_w, sg_b, w_ret_up, w_sg_up,
           w_out, ln1_g, ln1_b, w_gate_up, w_down, ln2_g, ln2_b):
    batch, seq, d = x.shape
    m = batch * seq
    assert w_ada.shape[0] == DEPTH == 1
    x2 = x.reshape(m, d)

    c_pad = jnp.zeros((8, d), F32).at[:batch].set(c)
    ada = _ada(c_pad, w_ada[0], b_ada[0][None, :])[:batch]
    sh1, sc1, g1, sh2, sc2, g2 = [t[:, None, :] for t in jnp.split(ada, 6, axis=-1)]

    h1 = _modulate(x2, sc1, sh1, seq)

    inv_freq = (ROPE_BASE ** (-jnp.arange(0, HEAD_DIM, 2, dtype=F32) / HEAD_DIM))[None, :]
    cos, sin = _rope_tables(positions.astype(F32).reshape(m, 1), inv_freq)

    z = _zproj(h1, w_in[0].astype(BF16), cos, sin)

    chunk_decay, inner_decay, xi_b, zeta_b = _decay_tables()
    ret = _retention(z, chunk_decay, inner_decay, xi_b, zeta_b, batch, seq)
    sgu = _sgu(z, sg_ln_g[0][None, :], sg_ln_b[0][None, :], sg_w[0], sg_b[0][:, :, None])

    gate_col0 = 4 * RET_HEADS * HEAD_DIM + 2 * SG_GROUPS * SG_GROUP_DIM
    mix = _mix(ret, sgu, w_ret_up[0].astype(BF16), w_sg_up[0].astype(BF16), z, gate_col0)
    branch1 = _outproj(mix, w_out[0].astype(BF16))
    x1, h2 = _ln_call(_ln1_kernel, "ln1", x2, branch1, [g1, sc2, sh2],
                      [ln1_g[0][None, :], ln1_b[0][None, :]], [F32, BF16], seq)

    ffn = _ffn(h2, w_gate_up[0].astype(BF16), w_down[0].astype(BF16))
    (out,) = _ln_call(_ln2_kernel, "ln2", x1, ffn, [g2],
                      [ln2_g[0][None, :], ln2_b[0][None, :]], [F32], seq)
    return out.reshape(batch, seq, d)
```

```python
import functools

import jax
import jax.numpy as jnp
from jax import lax
from jax.experimental import pallas as pl
from jax.experimental.pallas import tpu as pltpu

F32 = jnp.float32
BF16 = jnp.bfloat16

RET_HEADS = 8
HEAD_DIM = 256
SG_GROUPS = 8
SG_GROUP_DIM = 256
CHUNK = 128
ROPE_BASE = 10000.0
LN_EPS = 1e-5
DEPTH = 1
ALPHA = (2.0 * DEPTH) ** 0.25

MIB = 1024 * 1024
V7X_VMEM_BUDGET = 56 * MIB


def _params(sem, vmem_bytes):
    return pltpu.CompilerParams(dimension_semantics=sem, vmem_limit_bytes=vmem_bytes)


class _HostedCast:
    def __init__(self, weight, rows_per_block, first_step, step_of):
        rows, cols = weight.shape
        self.weight = weight
        self.first_step = first_step
        self.num_blocks = rows // rows_per_block
        assert self.num_blocks * rows_per_block == rows
        last = self.num_blocks - 1

        def index_map(*grid_ids):
            return (jnp.clip(step_of(*grid_ids) - first_step, 0, last), 0)

        self.spec = pl.BlockSpec((rows_per_block, cols), index_map)
        self.out_shape = jax.ShapeDtypeStruct((rows, cols), BF16)

    def run(self, step, src_ref, dst_ref):
        @pl.when((step >= self.first_step) & (step < self.first_step + self.num_blocks))
        def _():
            dst_ref[...] = src_ref[...].astype(BF16)


ADA_TN = 512


def _ada_kernel(c_ref, w_ref, b_ref, o_ref):
    s = jax.nn.silu(c_ref[...]).astype(BF16)
    o_ref[...] = jnp.dot(s, w_ref[...].astype(BF16), preferred_element_type=F32) + b_ref[...]


def _ada(c_pad, w_ada, b_ada):
    rows, d = c_pad.shape
    n = w_ada.shape[1]
    return pl.pallas_call(
        _ada_kernel,
        grid=(n // ADA_TN,),
        in_specs=[pl.BlockSpec((rows, d), lambda j: (0, 0)),
                  pl.BlockSpec((d, ADA_TN), lambda j: (0, j)),
                  pl.BlockSpec((1, ADA_TN), lambda j: (0, j))],
        out_specs=pl.BlockSpec((rows, ADA_TN), lambda j: (0, j)),
        out_shape=jax.ShapeDtypeStruct((rows, n), F32),
        compiler_params=_params(("arbitrary",), 32 * MIB),
        name="ada",
    )(c_pad, w_ada, b_ada)


MOD_TR = 512


def _modulate_kernel(x_ref, sc_ref, sh_ref, o_ref):
    o_ref[...] = (x_ref[...] * (1.0 + sc_ref[0]) + sh_ref[0]).astype(BF16)


def _modulate(x2, sc, sh, seq):
    m, d = x2.shape
    per_batch = seq // MOD_TR
    vec = pl.BlockSpec((1, 1, d), lambda i: (i // per_batch, 0, 0))
    return pl.pallas_call(
        _modulate_kernel,
        grid=(m // MOD_TR,),
        in_specs=[pl.BlockSpec((MOD_TR, d), lambda i: (i, 0)), vec, vec],
        out_specs=pl.BlockSpec((MOD_TR, d), lambda i: (i, 0)),
        out_shape=jax.ShapeDtypeStruct((m, d), BF16),
        compiler_params=_params(("arbitrary",), 40 * MIB),
        name="modulate",
    )(x2, sc, sh)


ROPE_TR = 1024


def _rope_kernel(pos_ref, inv_ref, cos_ref, sin_ref):
    ang = pos_ref[...] * inv_ref[...]
    cos_ref[...] = jnp.cos(ang)
    sin_ref[...] = jnp.sin(ang)


def _rope_tables(pos_col, inv_freq):
    m = pos_col.shape[0]
    half = inv_freq.shape[1]
    out = jax.ShapeDtypeStruct((m, half), F32)
    return pl.pallas_call(
        _rope_kernel,
        grid=(m // ROPE_TR,),
        in_specs=[pl.BlockSpec((ROPE_TR, 1), lambda i: (i, 0)),
                  pl.BlockSpec((1, half), lambda i: (0, 0))],
        out_specs=[pl.BlockSpec((ROPE_TR, half), lambda i: (i, 0))] * 2,
        out_shape=[out, out],
        compiler_params=_params(("arbitrary",), 32 * MIB),
        name="rope",
    )(pos_col, inv_freq)


Z_TM = 1024
Z_TN = 1024
Z_RC = 256


def _zproj_kernel(*refs, nt, mt, bounds, hosted):
    nh = len(hosted)
    h_ref, w_ref, cos_ref, sin_ref = refs[:4]
    hosted_src = refs[4:4 + nh]
    o_ref = refs[4 + nh]
    hosted_dst = refs[5 + nh:5 + 2 * nh]
    wbf_ref = refs[5 + 2 * nh]
    n = pl.program_id(0)
    m = pl.program_id(1)
    half = HEAD_DIM // 2
    kc = w_ref.shape[0]

    for cast, src, dst in zip(hosted, hosted_src, hosted_dst):
        cast.run(n * mt + m, src, dst)

    @pl.when(n < nt)
    def _stage():
        wbf_ref[n % 2, pl.ds(pl.multiple_of(m * kc, kc), kc), :] = w_ref[...].astype(BF16)

    @pl.when(n == 0)
    def _no_tile_yet():
        o_ref[...] = jnp.zeros_like(o_ref)

    t = n - 1
    slot = (n + 1) % 2

    def run(epilogue):
        for r in range(Z_TM // Z_RC):
            rows = slice(r * Z_RC, (r + 1) * Z_RC)
            acc = jnp.dot(h_ref[rows, :], wbf_ref[slot], preferred_element_type=F32)
            epilogue(acc, rows)

    def rotary(acc, rows):
        scale = jnp.where(t < q_end, 1.0, HEAD_DIM ** -0.5).astype(F32)
        c = cos_ref[rows, :]
        s = sin_ref[rows, :]
        for j in range(Z_TN // HEAD_DIM):
            x1 = acc[:, j * HEAD_DIM: j * HEAD_DIM + half]
            x2 = acc[:, j * HEAD_DIM + half: (j + 1) * HEAD_DIM]
            o_ref[rows, j * HEAD_DIM: j * HEAD_DIM + half] = ((x1 * c - x2 * s) * scale).astype(BF16)
            o_ref[rows, j * HEAD_DIM + half: (j + 1) * HEAD_DIM] = ((x2 * c + x1 * s) * scale).astype(BF16)

    def pointwise(fn):
        def epi(acc, rows):
            o_ref[rows, :] = fn(acc).astype(BF16)
        return epi

    q_end, k_end, v_end, g_end, us_end = bounds
    pl.when((n >= 1) & (t < k_end))(lambda: run(rotary))
    pl.when((t >= k_end) & (t < v_end))(lambda: run(pointwise(lambda a: a)))
    pl.when((t >= v_end) & (t < g_end))(lambda: run(pointwise(jax.nn.silu)))
    pl.when((t >= g_end) & (t < us_end))(lambda: run(pointwise(jax.nn.gelu)))
    pl.when(t >= us_end)(lambda: run(pointwise(jax.nn.sigmoid)))


def _zproj(h1, w_in, cos, sin, hosted_weights):
    m, d = h1.shape
    n = w_in.shape[1]
    nt, mt = n // Z_TN, m // Z_TM
    kc = d // mt
    ret_w = RET_HEADS * HEAD_DIM
    sg_w = SG_GROUPS * SG_GROUP_DIM
    bounds = tuple(b // Z_TN for b in (ret_w, 2 * ret_w, 3 * ret_w, 4 * ret_w, 4 * ret_w + 2 * sg_w))
    half = cos.shape[1]
    hosted = [_HostedCast(w, rb, first, lambda j, i: j * mt + i) for w, rb, first in hosted_weights]
    assert all(c.first_step + c.num_blocks <= (nt + 1) * mt for c in hosted)

    def row_tile(j, i):
        return (jnp.where(j == 0, 0, i), 0)

    outs = pl.pallas_call(
        functools.partial(_zproj_kernel, nt=nt, mt=mt, bounds=bounds, hosted=hosted),
        grid=(nt + 1, mt),
        in_specs=[pl.BlockSpec((Z_TM, d), row_tile),
                  pl.BlockSpec((kc, Z_TN), lambda j, i: (i, jnp.minimum(j, nt - 1))),
                  pl.BlockSpec((Z_TM, half), row_tile),
                  pl.BlockSpec((Z_TM, half), row_tile)] + [c.spec for c in hosted],
        out_specs=[pl.BlockSpec((Z_TM, Z_TN), lambda j, i: (i, jnp.maximum(j - 1, 0)))]
                  + [c.spec for c in hosted],
        out_shape=[jax.ShapeDtypeStruct((m, n), BF16)] + [c.out_shape for c in hosted],
        scratch_shapes=[pltpu.VMEM((2, d, Z_TN), BF16)],
        compiler_params=_params(("arbitrary", "arbitrary"), V7X_VMEM_BUDGET),
        name="zproj",
    )(h1, w_in, cos, sin, *[c.weight for c in hosted])
    return outs[0], outs[1:]


def _retention_kernel(cd_ref, q_ref, k_ref, v_ref, g_ref, dec_ref, xi_ref, zeta_ref, o_ref, state_ref):
    @pl.when(pl.program_id(1) == 0)
    def _():
        state_ref[...] = jnp.zeros_like(state_ref)

    for h in range(RET_HEADS):
        cols = slice(h * HEAD_DIM, (h + 1) * HEAD_DIM)
        q = q_ref[:, cols]
        k = k_ref[:, cols]
        v = v_ref[:, cols]
        scores = lax.dot_general(q, k, (((1,), (1,)), ((), ())), preferred_element_type=F32)
        scores = scores * dec_ref[h]
        inner = jnp.dot(scores.astype(BF16), v, preferred_element_type=F32)
        state = state_ref[h]
        cross = jnp.dot(q, state.astype(BF16), preferred_element_type=F32) * xi_ref[h]
        kz = (k.astype(F32) * zeta_ref[h]).astype(BF16)
        update = lax.dot_general(kz, v, (((0,), (0,)), ((), ())), preferred_element_type=F32)
        state_ref[h] = cd_ref[h] * state + update
        y = inner + cross
        mu = jnp.mean(y, axis=-1, keepdims=True)
        yc = y - mu
        var = jnp.mean(yc * yc, axis=-1, keepdims=True)
        yn = yc * lax.rsqrt(var + LN_EPS)
        o_ref[:, cols] = (yn * g_ref[:, cols].astype(F32)).astype(BF16)


def _retention(z, chunk_decay, inner_decay, xi_b, zeta_b, batch, seq):
    m = z.shape[0]
    width = RET_HEADS * HEAD_DIM
    n_chunks = seq // CHUNK

    def zblock(col):
        return pl.BlockSpec((CHUNK, width), lambda b, c: (b * n_chunks + c, col))

    def const(shape):
        return pl.BlockSpec(shape, lambda b, c: (0,) * len(shape))

    return pl.pallas_call(
        _retention_kernel,
        grid=(batch, n_chunks),
        in_specs=[pl.BlockSpec(memory_space=pltpu.SMEM),
                  zblock(0), zblock(1), zblock(2), zblock(3),
                  const(inner_decay.shape), const(xi_b.shape), const(zeta_b.shape)],
        out_specs=pl.BlockSpec((CHUNK, width), lambda b, c: (b * n_chunks + c, 0)),
        out_shape=jax.ShapeDtypeStruct((m, width), BF16),
        scratch_shapes=[pltpu.VMEM((RET_HEADS, HEAD_DIM, HEAD_DIM), F32)],
        compiler_params=_params(("arbitrary", "arbitrary"), 32 * MIB),
        name="retention",
    )(chunk_decay, z, z, z, z, inner_decay, xi_b, zeta_b)


def _sgu_kernel(u_ref, s_ref, lng_ref, lnb_ref, w_ref, b_ref, o_ref):
    s = s_ref[...].astype(F32)
    mu = jnp.mean(s, axis=-1, keepdims=True)
    sc = s - mu
    var = jnp.mean(sc * sc, axis=-1, keepdims=True)
    sn = (sc * lax.rsqrt(var + LN_EPS) * lng_ref[...] + lnb_ref[...]).astype(BF16)
    t_idx = lax.broadcasted_iota(jnp.int32, (CHUNK, CHUNK), 0)
    s_idx = lax.broadcasted_iota(jnp.int32, (CHUNK, CHUNK), 1)
    causal = t_idx >= s_idx
    for g in range(SG_GROUPS):
        cols = slice(g * SG_GROUP_DIM, (g + 1) * SG_GROUP_DIM)
        w = jnp.where(causal, w_ref[g], 0.0).astype(BF16)
        mixed = jnp.dot(w, sn[:, cols], preferred_element_type=F32) + b_ref[g]
        o_ref[:, cols] = (u_ref[:, cols].astype(F32) * mixed).astype(BF16)


def _sgu(z, ln_g, ln_b, w_s, b_col):
    m = z.shape[0]
    width = SG_GROUPS * SG_GROUP_DIM
    return pl.pallas_call(
        _sgu_kernel,
        grid=(m // CHUNK,),
        in_specs=[pl.BlockSpec((CHUNK, width), lambda i: (i, 4)),
                  pl.BlockSpec((CHUNK, width), lambda i: (i, 5)),
                  pl.BlockSpec((1, width), lambda i: (0, 0)),
                  pl.BlockSpec((1, width), lambda i: (0, 0)),
                  pl.BlockSpec(w_s.shape, lambda i: (0, 0, 0)),
                  pl.BlockSpec(b_col.shape, lambda i: (0, 0, 0))],
        out_specs=pl.BlockSpec((CHUNK, width), lambda i: (i, 0)),
        out_shape=jax.ShapeDtypeStruct((m, width), BF16),
        compiler_params=_params(("arbitrary",), 32 * MIB),
        name="sgu",
    )(z, z, ln_g, ln_b, w_s, b_col)


MIX_TM = 1024
MIX_TN = 512
MIX_RC = 256


def _mix_kernel(ret_ref, sgu_ref, wr_ref, ws_ref, gr_ref, gs_ref, cast_src, o_ref, cast_dst, *, nt, hosted):
    hosted.run(pl.program_id(0) * nt + pl.program_id(1), cast_src, cast_dst)
    for r in range(MIX_TM // MIX_RC):
        rows = slice(r * MIX_RC, (r + 1) * MIX_RC)
        y_ret = jnp.dot(ret_ref[rows, :], wr_ref[...], preferred_element_type=F32)
        y_sg = jnp.dot(sgu_ref[rows, :], ws_ref[...], preferred_element_type=F32)
        mix = gr_ref[rows, :].astype(F32) * y_ret + gs_ref[rows, :].astype(F32) * y_sg
        o_ref[rows, :] = mix.astype(BF16)


def _mix(ret, sgu, w_ret_up, w_sg_up, z, gate_col0, hosted_weight, hosted_rows):
    m, kdim = ret.shape
    n = w_ret_up.shape[1]
    nt = n // MIX_TN
    g0 = gate_col0 // MIX_TN
    g1 = g0 + nt
    hosted = _HostedCast(hosted_weight, hosted_rows, 0, lambda i, j: i * nt + j)
    assert hosted.num_blocks <= (m // MIX_TM) * nt
    return pl.pallas_call(
        functools.partial(_mix_kernel, nt=nt, hosted=hosted),
        grid=(m // MIX_TM, nt),
        in_specs=[pl.BlockSpec((MIX_TM, kdim), lambda i, j: (i, 0)),
                  pl.BlockSpec((MIX_TM, kdim), lambda i, j: (i, 0)),
                  pl.BlockSpec((kdim, MIX_TN), lambda i, j: (0, j)),
                  pl.BlockSpec((kdim, MIX_TN), lambda i, j: (0, j)),
                  pl.BlockSpec((MIX_TM, MIX_TN), lambda i, j: (i, g0 + j)),
                  pl.BlockSpec((MIX_TM, MIX_TN), lambda i, j: (i, g1 + j)),
                  hosted.spec],
        out_specs=[pl.BlockSpec((MIX_TM, MIX_TN), lambda i, j: (i, j)), hosted.spec],
        out_shape=[jax.ShapeDtypeStruct((m, n), BF16), hosted.out_shape],
        compiler_params=_params(("arbitrary", "arbitrary"), 48 * MIB),
        name="mix",
    )(ret, sgu, w_ret_up, w_sg_up, z, z, hosted_weight)


OUT_TM = 1024
OUT_TN = 1024
OUT_RC = 256


def _outproj_kernel(a_ref, w_ref, o_ref):
    for r in range(OUT_TM // OUT_RC):
        rows = slice(r * OUT_RC, (r + 1) * OUT_RC)
        o_ref[rows, :] = jnp.dot(a_ref[rows, :], w_ref[...], preferred_element_type=F32)


def _outproj(a, w):
    m, kdim = a.shape
    n = w.shape[1]
    return pl.pallas_call(
        _outproj_kernel,
        grid=(m // OUT_TM, n // OUT_TN),
        in_specs=[pl.BlockSpec((OUT_TM, kdim), lambda i, j: (i, 0)),
                  pl.BlockSpec((kdim, OUT_TN), lambda i, j: (0, j))],
        out_specs=pl.BlockSpec((OUT_TM, OUT_TN), lambda i, j: (i, j)),
        out_shape=jax.ShapeDtypeStruct((m, n), F32),
        compiler_params=_params(("arbitrary", "arbitrary"), 48 * MIB),
        name="outproj",
    )(a, w)


LN_TR = 256


def _residual_ln(x, branch, gate, ln_g, ln_b):
    y = ALPHA * x + (1.0 + gate) * branch
    mu = jnp.mean(y, axis=-1, keepdims=True)
    yc = y - mu
    var = jnp.mean(yc * yc, axis=-1, keepdims=True)
    return yc * lax.rsqrt(var + LN_EPS) * ln_g + ln_b


def _ln1_kernel(x_ref, br_ref, gate_ref, lng_ref, lnb_ref, sc_ref, sh_ref, x1_ref, h2_ref):
    x1 = _residual_ln(x_ref[...], br_ref[...], gate_ref[0], lng_ref[...], lnb_ref[...])
    x1_ref[...] = x1
    h2_ref[...] = (x1 * (1.0 + sc_ref[0]) + sh_ref[0]).astype(BF16)


def _ln2_kernel(x_ref, br_ref, gate_ref, lng_ref, lnb_ref, o_ref):
    o_ref[...] = _residual_ln(x_ref[...], br_ref[...], gate_ref[0], lng_ref[...], lnb_ref[...])


def _ln_call(kernel, name, x, branch, batch_vecs, row_vecs, out_dtypes, seq):
    m, d = x.shape
    per_batch = seq // LN_TR
    tile = pl.BlockSpec((LN_TR, d), lambda i: (i, 0))
    bvec = pl.BlockSpec((1, 1, d), lambda i: (i // per_batch, 0, 0))
    rvec = pl.BlockSpec((1, d), lambda i: (0, 0))
    operands = [x, branch, batch_vecs[0], *row_vecs, *batch_vecs[1:]]
    specs = [tile, tile, bvec] + [rvec] * len(row_vecs) + [bvec] * (len(batch_vecs) - 1)
    return pl.pallas_call(
        kernel,
        grid=(m // LN_TR,),
        in_specs=specs,
        out_specs=[tile] * len(out_dtypes),
        out_shape=[jax.ShapeDtypeStruct((m, d), dt) for dt in out_dtypes],
        compiler_params=_params(("arbitrary",), 48 * MIB),
        name=name,
    )(*operands)


FFN_TM = 1024
FFN_TF = 256
FFN_RC = 256
RESIDENT = pl.Buffered(1)


def _ffn_kernel(h_ref, wg_ref, wu_ref, wd_ref, o_ref):
    f = pl.program_id(1)

    def step(first):
        for r in range(FFN_TM // FFN_RC):
            rows = slice(r * FFN_RC, (r + 1) * FFN_RC)
            h = h_ref[rows, :]
            a_gate = jnp.dot(h, wg_ref[...], preferred_element_type=F32)
            a_up = jnp.dot(h, wu_ref[...], preferred_element_type=F32)
            act = (jax.nn.silu(a_gate) * a_up).astype(BF16)
            contrib = jnp.dot(act, wd_ref[...], preferred_element_type=F32)
            if first:
                o_ref[rows, :] = contrib
            else:
                o_ref[rows, :] += contrib

    pl.when(f == 0)(lambda: step(True))
    pl.when(f != 0)(lambda: step(False))


def _ffn(h2, w_gate_up, w_down):
    m, d = h2.shape
    d_ff = w_down.shape[0]
    nf = d_ff // FFN_TF
    return pl.pallas_call(
        _ffn_kernel,
        grid=(m // FFN_TM, nf),
        in_specs=[pl.BlockSpec((FFN_TM, d), lambda i, f: (i, 0), pipeline_mode=RESIDENT),
                  pl.BlockSpec((d, FFN_TF), lambda i, f: (0, f)),
                  pl.BlockSpec((d, FFN_TF), lambda i, f: (0, nf + f)),
                  pl.BlockSpec((FFN_TF, d), lambda i, f: (f, 0))],
        out_specs=pl.BlockSpec((FFN_TM, d), lambda i, f: (i, 0), pipeline_mode=RESIDENT),
        out_shape=jax.ShapeDtypeStruct((m, d), F32),
        compiler_params=_params(("arbitrary", "arbitrary"), 48 * MIB),
        name="ffn",
    )(h2, w_gate_up, w_gate_up, w_down)


def _decay_tables():
    log_g = jnp.log(1.0 - 2.0 ** (-5.0 - jnp.arange(RET_HEADS, dtype=F32)))
    idx = jnp.arange(CHUNK, dtype=F32)
    diff = idx[:, None] - idx[None, :]
    inner = jnp.where((diff >= 0)[None],
                      jnp.exp(log_g[:, None, None] * jnp.maximum(diff, 0.0)[None]), 0.0)
    xi = jnp.exp(log_g[:, None] * (idx[None, :] + 1.0))
    zeta = jnp.exp(log_g[:, None] * (CHUNK - 1.0 - idx[None, :]))
    chunk_decay = jnp.exp(log_g * CHUNK)
    xi_b = jnp.broadcast_to(xi[:, :, None], (RET_HEADS, CHUNK, HEAD_DIM))
    zeta_b = jnp.broadcast_to(zeta[:, :, None], (RET_HEADS, CHUNK, HEAD_DIM))
    return chunk_decay, inner, xi_b, zeta_b


def kernel(x, c, positions, w_ada, b_ada, w_in, sg_ln_g, sg_ln_b, sg_w, sg_b, w_ret_up, w_sg_up,
           w_out, ln1_g, ln1_b, w_gate_up, w_down, ln2_g, ln2_b):
    batch, seq, d = x.shape
    m = batch * seq
    assert w_ada.shape[0] == DEPTH == 1
    x2 = x.reshape(m, d)

    c_pad = jnp.zeros((8, d), F32).at[:batch].set(c)
    ada = _ada(c_pad, w_ada[0], b_ada[0][None, :])[:batch]
    sh1, sc1, g1, sh2, sc2, g2 = [t[:, None, :] for t in jnp.split(ada, 6, axis=-1)]

    h1 = _modulate(x2, sc1, sh1, seq)

    inv_freq = (ROPE_BASE ** (-jnp.arange(0, HEAD_DIM, 2, dtype=F32) / HEAD_DIM))[None, :]
    cos, sin = _rope_tables(positions.astype(F32).reshape(m, 1), inv_freq)

    z, (w_gate_up_bf, w_out_bf, w_ret_up_bf, w_sg_up_bf) = _zproj(
        h1, w_in[0], cos, sin,
        [(w_gate_up[0], 32, 0), (w_out[0], 32, 0), (w_ret_up[0], 16, 40), (w_sg_up[0], 16, 40)])

    chunk_decay, inner_decay, xi_b, zeta_b = _decay_tables()
    ret = _retention(z, chunk_decay, inner_decay, xi_b, zeta_b, batch, seq)
    sgu = _sgu(z, sg_ln_g[0][None, :], sg_ln_b[0][None, :], sg_w[0], sg_b[0][:, :, None])

    gate_col0 = 4 * RET_HEADS * HEAD_DIM + 2 * SG_GROUPS * SG_GROUP_DIM
    mix, w_down_bf = _mix(ret, sgu, w_ret_up_bf, w_sg_up_bf, z, gate_col0, w_down[0], 256)
    branch1 = _outproj(mix, w_out_bf)
    x1, h2 = _ln_call(_ln1_kernel, "ln1", x2, branch1, [g1, sc2, sh2],
                      [ln1_g[0][None, :], ln1_b[0][None, :]], [F32, BF16], seq)

    ffn = _ffn(h2, w_gate_up_bf, w_down_bf)
    (out,) = _ln_call(_ln2_kernel, "ln2", x1, ffn, [g2],
                      [ln2_g[0][None, :], ln2_b[0][None, :]], [F32], seq)
    return out.reshape(batch, seq, d)
```

```python
import functools

import jax
import jax.numpy as jnp
from jax import lax
from jax.experimental import pallas as pl
from jax.experimental.pallas import tpu as pltpu

F32 = jnp.float32
BF16 = jnp.bfloat16

RET_HEADS = 8
HEAD_DIM = 256
SG_GROUPS = 8
SG_GROUP_DIM = 256
CHUNK = 128
ROPE_BASE = 10000.0
LN_EPS = 1e-5
DEPTH = 1
ALPHA = (2.0 * DEPTH) ** 0.25

MIB = 1024 * 1024
V7X_VMEM_BUDGET = 56 * MIB


def _params(sem, vmem_bytes):
    return pltpu.CompilerParams(dimension_semantics=sem, vmem_limit_bytes=vmem_bytes)


class _HostedCast:
    def __init__(self, weight, rows_per_block, first_step, step_of):
        rows, cols = weight.shape
        self.weight = weight
        self.first_step = first_step
        self.num_blocks = rows // rows_per_block
        assert self.num_blocks * rows_per_block == rows
        last = self.num_blocks - 1

        def index_map(*grid_ids):
            return (jnp.clip(step_of(*grid_ids) - first_step, 0, last), 0)

        self.spec = pl.BlockSpec((rows_per_block, cols), index_map)
        self.out_shape = jax.ShapeDtypeStruct((rows, cols), BF16)

    @staticmethod
    def run(src_ref, dst_ref):
        dst_ref[...] = src_ref[...].astype(BF16)


ADA_TN = 512


def _ada_kernel(c_ref, w_ref, b_ref, o_ref):
    s = jax.nn.silu(c_ref[...]).astype(BF16)
    o_ref[...] = jnp.dot(s, w_ref[...].astype(BF16), preferred_element_type=F32) + b_ref[...]


def _ada(c_pad, w_ada, b_ada):
    rows, d = c_pad.shape
    n = w_ada.shape[1]
    return pl.pallas_call(
        _ada_kernel,
        grid=(n // ADA_TN,),
        in_specs=[pl.BlockSpec((rows, d), lambda j: (0, 0)),
                  pl.BlockSpec((d, ADA_TN), lambda j: (0, j)),
                  pl.BlockSpec((1, ADA_TN), lambda j: (0, j))],
        out_specs=pl.BlockSpec((rows, ADA_TN), lambda j: (0, j)),
        out_shape=jax.ShapeDtypeStruct((rows, n), F32),
        compiler_params=_params(("arbitrary",), 32 * MIB),
        name="ada",
    )(c_pad, w_ada, b_ada)


MOD_TR = 512


def _modulate_kernel(x_ref, sc_ref, sh_ref, o_ref):
    o_ref[...] = (x_ref[...] * (1.0 + sc_ref[0]) + sh_ref[0]).astype(BF16)


def _modulate(x2, sc, sh, seq):
    m, d = x2.shape
    per_batch = seq // MOD_TR
    vec = pl.BlockSpec((1, 1, d), lambda i: (i // per_batch, 0, 0))
    return pl.pallas_call(
        _modulate_kernel,
        grid=(m // MOD_TR,),
        in_specs=[pl.BlockSpec((MOD_TR, d), lambda i: (i, 0)), vec, vec],
        out_specs=pl.BlockSpec((MOD_TR, d), lambda i: (i, 0)),
        out_shape=jax.ShapeDtypeStruct((m, d), BF16),
        compiler_params=_params(("arbitrary",), 40 * MIB),
        name="modulate",
    )(x2, sc, sh)


ROPE_TR = 1024


def _rope_kernel(pos_ref, inv_ref, cos_ref, sin_ref):
    ang = pos_ref[...] * inv_ref[...]
    cos_ref[...] = jnp.cos(ang)
    sin_ref[...] = jnp.sin(ang)


def _rope_tables(pos_col, inv_freq):
    m = pos_col.shape[0]
    half = inv_freq.shape[1]
    out = jax.ShapeDtypeStruct((m, half), F32)
    return pl.pallas_call(
        _rope_kernel,
        grid=(m // ROPE_TR,),
        in_specs=[pl.BlockSpec((ROPE_TR, 1), lambda i: (i, 0)),
                  pl.BlockSpec((1, half), lambda i: (0, 0))],
        out_specs=[pl.BlockSpec((ROPE_TR, half), lambda i: (i, 0))] * 2,
        out_shape=[out, out],
        compiler_params=_params(("arbitrary",), 32 * MIB),
        name="rope",
    )(pos_col, inv_freq)


Z_TM = 1024
Z_TN = 1024
Z_RC = 256


def _zproj_kernel(*refs, nt, mt, bounds, hosted):
    nh = len(hosted)
    h_ref, w_ref, cos_ref, sin_ref = refs[:4]
    hosted_src = refs[4:4 + nh]
    o_ref = refs[4 + nh]
    hosted_dst = refs[5 + nh:5 + 2 * nh]
    wbf_ref = refs[5 + 2 * nh]
    n = pl.program_id(0)
    m = pl.program_id(1)
    half = HEAD_DIM // 2
    kc = w_ref.shape[0]

    def stage_and_host(stage_slot):
        wbf_ref[stage_slot, pl.ds(pl.multiple_of(m * kc, kc), kc), :] = w_ref[...].astype(BF16)
        for cast, src, dst in zip(hosted, hosted_src, hosted_dst):
            cast.run(src, dst)

    @pl.when(n == 0)
    def _no_tile_yet():
        stage_and_host(0)

    t = n - 1

    def run(epilogue):
        for stage_slot in (0, 1):
            @pl.when(n % 2 == stage_slot)
            def _():
                stage_and_host(stage_slot)
                for r in range(Z_TM // Z_RC):
                    rows = slice(r * Z_RC, (r + 1) * Z_RC)
                    acc = jnp.dot(h_ref[rows, :], wbf_ref[1 - stage_slot], preferred_element_type=F32)
                    epilogue(acc, rows)

    def rotary(acc, rows):
        scale = jnp.where(t < q_end, 1.0, HEAD_DIM ** -0.5).astype(F32)
        c = cos_ref[rows, :]
        s = sin_ref[rows, :]
        for j in range(Z_TN // HEAD_DIM):
            x1 = acc[:, j * HEAD_DIM: j * HEAD_DIM + half]
            x2 = acc[:, j * HEAD_DIM + half: (j + 1) * HEAD_DIM]
            o_ref[rows, j * HEAD_DIM: j * HEAD_DIM + half] = ((x1 * c - x2 * s) * scale).astype(BF16)
            o_ref[rows, j * HEAD_DIM + half: (j + 1) * HEAD_DIM] = ((x2 * c + x1 * s) * scale).astype(BF16)

    def pointwise(fn):
        def epi(acc, rows):
            o_ref[rows, :] = fn(acc).astype(BF16)
        return epi

    q_end, k_end, v_end, g_end, us_end = bounds
    pl.when((n >= 1) & (t < k_end))(lambda: run(rotary))
    pl.when((t >= k_end) & (t < v_end))(lambda: run(pointwise(lambda a: a)))
    pl.when((t >= v_end) & (t < g_end))(lambda: run(pointwise(jax.nn.silu)))
    pl.when((t >= g_end) & (t < us_end))(lambda: run(pointwise(jax.nn.gelu)))
    pl.when(t >= us_end)(lambda: run(pointwise(jax.nn.sigmoid)))


def _zproj(h1, w_in, cos, sin, hosted_weights):
    m, d = h1.shape
    n = w_in.shape[1]
    nt, mt = n // Z_TN, m // Z_TM
    kc = d // mt
    ret_w = RET_HEADS * HEAD_DIM
    sg_w = SG_GROUPS * SG_GROUP_DIM
    bounds = tuple(b // Z_TN for b in (ret_w, 2 * ret_w, 3 * ret_w, 4 * ret_w, 4 * ret_w + 2 * sg_w))
    half = cos.shape[1]
    hosted = [_HostedCast(w, rb, first, lambda j, i: j * mt + i) for w, rb, first in hosted_weights]
    assert all(c.first_step + c.num_blocks <= (nt + 1) * mt for c in hosted)

    def row_tile(j, i):
        return (jnp.where(j == 0, 0, i), 0)

    def out_tile(j, i):
        return (jnp.where(j == 0, 0, i), jnp.maximum(j - 1, 0))

    outs = pl.pallas_call(
        functools.partial(_zproj_kernel, nt=nt, mt=mt, bounds=bounds, hosted=hosted),
        grid=(nt + 1, mt),
        in_specs=[pl.BlockSpec((Z_TM, d), row_tile),
                  pl.BlockSpec((kc, Z_TN), lambda j, i: (i, jnp.minimum(j, nt - 1))),
                  pl.BlockSpec((Z_TM, half), row_tile),
                  pl.BlockSpec((Z_TM, half), row_tile)] + [c.spec for c in hosted],
        out_specs=[pl.BlockSpec((Z_TM, Z_TN), out_tile)]
                  + [c.spec for c in hosted],
        out_shape=[jax.ShapeDtypeStruct((m, n), BF16)] + [c.out_shape for c in hosted],
        scratch_shapes=[pltpu.VMEM((2, d, Z_TN), BF16)],
        compiler_params=_params(("arbitrary", "arbitrary"), V7X_VMEM_BUDGET),
        name="zproj",
    )(h1, w_in, cos, sin, *[c.weight for c in hosted])
    return outs[0], outs[1:]


def _retention_kernel(cd_ref, q_ref, k_ref, v_ref, g_ref, dec_ref, xi_ref, zeta_ref, o_ref, state_ref):
    @pl.when(pl.program_id(1) == 0)
    def _():
        state_ref[...] = jnp.zeros_like(state_ref)

    for h in range(RET_HEADS):
        cols = slice(h * HEAD_DIM, (h + 1) * HEAD_DIM)
        q = q_ref[:, cols]
        k = k_ref[:, cols]
        v = v_ref[:, cols]
        scores = lax.dot_general(q, k, (((1,), (1,)), ((), ())), preferred_element_type=F32)
        scores = scores * dec_ref[h]
        inner = jnp.dot(scores.astype(BF16), v, preferred_element_type=F32)
        state = state_ref[h]
        cross = jnp.dot(q, state.astype(BF16), preferred_element_type=F32) * xi_ref[h]
        kz = (k.astype(F32) * zeta_ref[h]).astype(BF16)
        update = lax.dot_general(kz, v, (((0,), (0,)), ((), ())), preferred_element_type=F32)
        state_ref[h] = cd_ref[h] * state + update
        y = inner + cross
        mu = jnp.mean(y, axis=-1, keepdims=True)
        yc = y - mu
        var = jnp.mean(yc * yc, axis=-1, keepdims=True)
        yn = yc * lax.rsqrt(var + LN_EPS)
        o_ref[:, cols] = (yn * g_ref[:, cols].astype(F32)).astype(BF16)


def _retention(z, chunk_decay, inner_decay, xi_b, zeta_b, batch, seq):
    m = z.shape[0]
    width = RET_HEADS * HEAD_DIM
    n_chunks = seq // CHUNK

    def zblock(col):
        return pl.BlockSpec((CHUNK, width), lambda b, c: (b * n_chunks + c, col))

    def const(shape):
        return pl.BlockSpec(shape, lambda b, c: (0,) * len(shape))

    return pl.pallas_call(
        _retention_kernel,
        grid=(batch, n_chunks),
        in_specs=[pl.BlockSpec(memory_space=pltpu.SMEM),
                  zblock(0), zblock(1), zblock(2), zblock(3),
                  const(inner_decay.shape), const(xi_b.shape), const(zeta_b.shape)],
        out_specs=pl.BlockSpec((CHUNK, width), lambda b, c: (b * n_chunks + c, 0)),
        out_shape=jax.ShapeDtypeStruct((m, width), BF16),
        scratch_shapes=[pltpu.VMEM((RET_HEADS, HEAD_DIM, HEAD_DIM), F32)],
        compiler_params=_params(("arbitrary", "arbitrary"), 32 * MIB),
        name="retention",
    )(chunk_decay, z, z, z, z, inner_decay, xi_b, zeta_b)


def _sgu_kernel(u_ref, s_ref, lng_ref, lnb_ref, w_ref, b_ref, o_ref):
    s = s_ref[...].astype(F32)
    mu = jnp.mean(s, axis=-1, keepdims=True)
    sc = s - mu
    var = jnp.mean(sc * sc, axis=-1, keepdims=True)
    sn = (sc * lax.rsqrt(var + LN_EPS) * lng_ref[...] + lnb_ref[...]).astype(BF16)
    t_idx = lax.broadcasted_iota(jnp.int32, (CHUNK, CHUNK), 0)
    s_idx = lax.broadcasted_iota(jnp.int32, (CHUNK, CHUNK), 1)
    causal = t_idx >= s_idx
    for g in range(SG_GROUPS):
        cols = slice(g * SG_GROUP_DIM, (g + 1) * SG_GROUP_DIM)
        w = jnp.where(causal, w_ref[g], 0.0).astype(BF16)
        mixed = jnp.dot(w, sn[:, cols], preferred_element_type=F32) + b_ref[g]
        o_ref[:, cols] = (u_ref[:, cols].astype(F32) * mixed).astype(BF16)


def _sgu(z, ln_g, ln_b, w_s, b_col):
    m = z.shape[0]
    width = SG_GROUPS * SG_GROUP_DIM
    return pl.pallas_call(
        _sgu_kernel,
        grid=(m // CHUNK,),
        in_specs=[pl.BlockSpec((CHUNK, width), lambda i: (i, 4)),
                  pl.BlockSpec((CHUNK, width), lambda i: (i, 5)),
                  pl.BlockSpec((1, width), lambda i: (0, 0)),
                  pl.BlockSpec((1, width), lambda i: (0, 0)),
                  pl.BlockSpec(w_s.shape, lambda i: (0, 0, 0)),
                  pl.BlockSpec(b_col.shape, lambda i: (0, 0, 0))],
        out_specs=pl.BlockSpec((CHUNK, width), lambda i: (i, 0)),
        out_shape=jax.ShapeDtypeStruct((m, width), BF16),
        compiler_params=_params(("arbitrary",), 32 * MIB),
        name="sgu",
    )(z, z, ln_g, ln_b, w_s, b_col)


MIX_TM = 1024
MIX_TN = 1024
MIX_RC = 256


def _mix_kernel(ret_ref, sgu_ref, wr_ref, ws_ref, gr_ref, gs_ref, o_ref):
    for r in range(MIX_TM // MIX_RC):
        rows = slice(r * MIX_RC, (r + 1) * MIX_RC)
        y_ret = jnp.dot(ret_ref[rows, :], wr_ref[...], preferred_element_type=F32)
        y_sg = jnp.dot(sgu_ref[rows, :], ws_ref[...], preferred_element_type=F32)
        mix = gr_ref[rows, :].astype(F32) * y_ret + gs_ref[rows, :].astype(F32) * y_sg
        o_ref[rows, :] = mix.astype(BF16)


def _mix(ret, sgu, w_ret_up, w_sg_up, z, gate_col0):
    m, kdim = ret.shape
    n = w_ret_up.shape[1]
    nt = n // MIX_TN
    g0 = gate_col0 // MIX_TN
    g1 = g0 + nt
    return pl.pallas_call(
        _mix_kernel,
        grid=(m // MIX_TM, nt),
        in_specs=[pl.BlockSpec((MIX_TM, kdim), lambda i, j: (i, 0)),
                  pl.BlockSpec((MIX_TM, kdim), lambda i, j: (i, 0)),
                  pl.BlockSpec((kdim, MIX_TN), lambda i, j: (0, j)),
                  pl.BlockSpec((kdim, MIX_TN), lambda i, j: (0, j)),
                  pl.BlockSpec((MIX_TM, MIX_TN), lambda i, j: (i, g0 + j)),
                  pl.BlockSpec((MIX_TM, MIX_TN), lambda i, j: (i, g1 + j))],
        out_specs=pl.BlockSpec((MIX_TM, MIX_TN), lambda i, j: (i, j)),
        out_shape=jax.ShapeDtypeStruct((m, n), BF16),
        compiler_params=_params(("arbitrary", "arbitrary"), V7X_VMEM_BUDGET),
        name="mix",
    )(ret, sgu, w_ret_up, w_sg_up, z, z)


OUT_TM = 1024
OUT_TN = 1024
OUT_RC = 256


def _outproj_kernel(a_ref, w_ref, o_ref):
    for r in range(OUT_TM // OUT_RC):
        rows = slice(r * OUT_RC, (r + 1) * OUT_RC)
        o_ref[rows, :] = jnp.dot(a_ref[rows, :], w_ref[...], preferred_element_type=F32).astype(BF16)


def _outproj(a, w):
    m, kdim = a.shape
    n = w.shape[1]
    return pl.pallas_call(
        _outproj_kernel,
        grid=(m // OUT_TM, n // OUT_TN),
        in_specs=[pl.BlockSpec((OUT_TM, kdim), lambda i, j: (i, 0)),
                  pl.BlockSpec((kdim, OUT_TN), lambda i, j: (0, j))],
        out_specs=pl.BlockSpec((OUT_TM, OUT_TN), lambda i, j: (i, j)),
        out_shape=jax.ShapeDtypeStruct((m, n), BF16),
        compiler_params=_params(("arbitrary", "arbitrary"), 48 * MIB),
        name="outproj",
    )(a, w)


LN_TR = 256


def _residual_ln(x, branch, gate, ln_g, ln_b):
    y = ALPHA * x + (1.0 + gate) * branch.astype(F32)
    mu = jnp.mean(y, axis=-1, keepdims=True)
    yc = y - mu
    var = jnp.mean(yc * yc, axis=-1, keepdims=True)
    return yc * lax.rsqrt(var + LN_EPS) * ln_g + ln_b


def _ln1_kernel(x_ref, br_ref, gate_ref, lng_ref, lnb_ref, sc_ref, sh_ref, x1_ref, h2_ref):
    x1 = _residual_ln(x_ref[...], br_ref[...], gate_ref[0], lng_ref[...], lnb_ref[...])
    x1_ref[...] = x1
    h2_ref[...] = (x1 * (1.0 + sc_ref[0]) + sh_ref[0]).astype(BF16)


def _ln2_kernel(x_ref, br_ref, gate_ref, lng_ref, lnb_ref, o_ref):
    o_ref[...] = _residual_ln(x_ref[...], br_ref[...], gate_ref[0], lng_ref[...], lnb_ref[...])


def _ln_call(kernel, name, x, branch, batch_vecs, row_vecs, out_dtypes, seq):
    m, d = x.shape
    per_batch = seq // LN_TR
    tile = pl.BlockSpec((LN_TR, d), lambda i: (i, 0))
    bvec = pl.BlockSpec((1, 1, d), lambda i: (i // per_batch, 0, 0))
    rvec = pl.BlockSpec((1, d), lambda i: (0, 0))
    operands = [x, branch, batch_vecs[0], *row_vecs, *batch_vecs[1:]]
    specs = [tile, tile, bvec] + [rvec] * len(row_vecs) + [bvec] * (len(batch_vecs) - 1)
    return pl.pallas_call(
        kernel,
        grid=(m // LN_TR,),
        in_specs=specs,
        out_specs=[tile] * len(out_dtypes),
        out_shape=[jax.ShapeDtypeStruct((m, d), dt) for dt in out_dtypes],
        compiler_params=_params(("arbitrary",), 48 * MIB),
        name=name,
    )(*operands)


UP_TM = 2048
UP_TF = 256
UP_RC = 512
DOWN_TM = 512
DOWN_TN = 1024
DOWN_RC = 256
RESIDENT = pl.Buffered(1)


def _ffn_up_kernel(h_ref, wg_ref, wu_ref, cast_src, o_ref, cast_dst):
    _HostedCast.run(cast_src, cast_dst)
    for r in range(UP_TM // UP_RC):
        rows = slice(r * UP_RC, (r + 1) * UP_RC)
        h = h_ref[rows, :]
        a_gate = jnp.dot(h, wg_ref[...], preferred_element_type=F32)
        a_up = jnp.dot(h, wu_ref[...], preferred_element_type=F32)
        o_ref[rows, :] = (jax.nn.silu(a_gate) * a_up).astype(BF16)


def _ffn_up(h2, w_gate_up, hosted_weight, hosted_rows):
    m, d = h2.shape
    d_ff = w_gate_up.shape[1] // 2
    nf = d_ff // UP_TF
    hosted = _HostedCast(hosted_weight, hosted_rows, 0, lambda i, f: i * nf + f)
    assert hosted.num_blocks <= (m // UP_TM) * nf
    return pl.pallas_call(
        _ffn_up_kernel,
        grid=(m // UP_TM, nf),
        in_specs=[pl.BlockSpec((UP_TM, d), lambda i, f: (i, 0), pipeline_mode=RESIDENT),
                  pl.BlockSpec((d, UP_TF), lambda i, f: (0, f)),
                  pl.BlockSpec((d, UP_TF), lambda i, f: (0, nf + f)),
                  hosted.spec],
        out_specs=[pl.BlockSpec((UP_TM, UP_TF), lambda i, f: (i, f)), hosted.spec],
        out_shape=[jax.ShapeDtypeStruct((m, d_ff), BF16), hosted.out_shape],
        compiler_params=_params(("arbitrary", "arbitrary"), 40 * MIB),
        name="ffn_up",
    )(h2, w_gate_up, w_gate_up, hosted_weight)


def _ffn_down_kernel(a_ref, w_ref, o_ref):
    for r in range(DOWN_TM // DOWN_RC):
        rows = slice(r * DOWN_RC, (r + 1) * DOWN_RC)
        o_ref[rows, :] = jnp.dot(a_ref[rows, :], w_ref[...], preferred_element_type=F32).astype(BF16)


def _ffn_down(act, w_down):
    m, d_ff = act.shape
    d = w_down.shape[1]
    return pl.pallas_call(
        _ffn_down_kernel,
        grid=(d // DOWN_TN, m // DOWN_TM),
        in_specs=[pl.BlockSpec((DOWN_TM, d_ff), lambda j, i: (i, 0)),
                  pl.BlockSpec((d_ff, DOWN_TN), lambda j, i: (0, j), pipeline_mode=RESIDENT)],
        out_specs=pl.BlockSpec((DOWN_TM, DOWN_TN), lambda j, i: (i, j)),
        out_shape=jax.ShapeDtypeStruct((m, d), BF16),
        compiler_params=_params(("arbitrary", "arbitrary"), V7X_VMEM_BUDGET),
        name="ffn_down",
    )(act, w_down)


def _decay_tables():
    log_g = jnp.log(1.0 - 2.0 ** (-5.0 - jnp.arange(RET_HEADS, dtype=F32)))
    idx = jnp.arange(CHUNK, dtype=F32)
    diff = idx[:, None] - idx[None, :]
    inner = jnp.where((diff >= 0)[None],
                      jnp.exp(log_g[:, None, None] * jnp.maximum(diff, 0.0)[None]), 0.0)
    xi = jnp.exp(log_g[:, None] * (idx[None, :] + 1.0))
    zeta = jnp.exp(log_g[:, None] * (CHUNK - 1.0 - idx[None, :]))
    chunk_decay = jnp.exp(log_g * CHUNK)
    xi_b = jnp.broadcast_to(xi[:, :, None], (RET_HEADS, CHUNK, HEAD_DIM))
    zeta_b = jnp.broadcast_to(zeta[:, :, None], (RET_HEADS, CHUNK, HEAD_DIM))
    return chunk_decay, inner, xi_b, zeta_b


def kernel(x, c, positions, w_ada, b_ada, w_in, sg_ln_g, sg_ln_b, sg_w, sg_b, w_ret_up, w_sg_up,
           w_out, ln1_g, ln1_b, w_gate_up, w_down, ln2_g, ln2_b):
    batch, seq, d = x.shape
    m = batch * seq
    assert w_ada.shape[0] == DEPTH == 1
    x2 = x.reshape(m, d)

    c_pad = jnp.zeros((8, d), F32).at[:batch].set(c)
    ada = _ada(c_pad, w_ada[0], b_ada[0][None, :])[:batch]
    sh1, sc1, g1, sh2, sc2, g2 = [t[:, None, :] for t in jnp.split(ada, 6, axis=-1)]

    h1 = _modulate(x2, sc1, sh1, seq)

    inv_freq = (ROPE_BASE ** (-jnp.arange(0, HEAD_DIM, 2, dtype=F32) / HEAD_DIM))[None, :]
    cos, sin = _rope_tables(positions.astype(F32).reshape(m, 1), inv_freq)

    z, (w_gate_up_bf, w_out_bf, w_ret_up_bf, w_sg_up_bf) = _zproj(
        h1, w_in[0], cos, sin,
        [(w_gate_up[0], 32, 0), (w_out[0], 32, 0), (w_ret_up[0], 16, 40), (w_sg_up[0], 16, 40)])

    chunk_decay, inner_decay, xi_b, zeta_b = _decay_tables()
    ret = _retention(z, chunk_decay, inner_decay, xi_b, zeta_b, batch, seq)
    sgu = _sgu(z, sg_ln_g[0][None, :], sg_ln_b[0][None, :], sg_w[0], sg_b[0][:, :, None])

    gate_col0 = 4 * RET_HEADS * HEAD_DIM + 2 * SG_GROUPS * SG_GROUP_DIM
    mix = _mix(ret, sgu, w_ret_up_bf, w_sg_up_bf, z, gate_col0)
    branch1 = _outproj(mix, w_out_bf)
    x1, h2 = _ln_call(_ln1_kernel, "ln1", x2, branch1, [g1, sc2, sh2],
                      [ln1_g[0][None, :], ln1_b[0][None, :]], [F32, BF16], seq)

    act, w_down_bf = _ffn_up(h2, w_gate_up_bf, w_down[0], 64)
    ffn = _ffn_down(act, w_down_bf)
    (out,) = _ln_call(_ln2_kernel, "ln2", x1, ffn, [g2],
                      [ln2_g[0][None, :], ln2_b[0][None, :]], [F32], seq)
    return out.reshape(batch, seq, d)
```

```python
import functools

import jax
import jax.numpy as jnp
from jax import lax
from jax.experimental import pallas as pl
from jax.experimental.pallas import tpu as pltpu

F32 = jnp.float32
BF16 = jnp.bfloat16

RET_HEADS = 8
HEAD_DIM = 256
SG_GROUPS = 8
SG_GROUP_DIM = 256
CHUNK = 128
ROPE_BASE = 10000.0
LN_EPS = 1e-5
DEPTH = 1
ALPHA = (2.0 * DEPTH) ** 0.25

MIB = 1024 * 1024
V7X_VMEM_BUDGET = 56 * MIB


def _params(sem, vmem_bytes):
    return pltpu.CompilerParams(dimension_semantics=sem, vmem_limit_bytes=vmem_bytes)


class _HostedCast:
    def __init__(self, weight, rows_per_block, first_step, step_of):
        rows, cols = weight.shape
        self.weight = weight
        self.first_step = first_step
        self.num_blocks = rows // rows_per_block
        assert self.num_blocks * rows_per_block == rows
        last = self.num_blocks - 1

        def index_map(*grid_ids):
            return (jnp.clip(step_of(*grid_ids) - first_step, 0, last), 0)

        self.spec = pl.BlockSpec((rows_per_block, cols), index_map)
        self.out_shape = jax.ShapeDtypeStruct((rows, cols), BF16)

    @staticmethod
    def run(src_ref, dst_ref):
        dst_ref[...] = src_ref[...].astype(BF16)


ADA_TN = 512


def _ada_kernel(c_ref, w_ref, b_ref, o_ref):
    s = jax.nn.silu(c_ref[...]).astype(BF16)
    o_ref[...] = jnp.dot(s, w_ref[...].astype(BF16), preferred_element_type=F32) + b_ref[...]


def _ada(c_pad, w_ada, b_ada):
    rows, d = c_pad.shape
    n = w_ada.shape[1]
    return pl.pallas_call(
        _ada_kernel,
        grid=(n // ADA_TN,),
        in_specs=[pl.BlockSpec((rows, d), lambda j: (0, 0)),
                  pl.BlockSpec((d, ADA_TN), lambda j: (0, j)),
                  pl.BlockSpec((1, ADA_TN), lambda j: (0, j))],
        out_specs=pl.BlockSpec((rows, ADA_TN), lambda j: (0, j)),
        out_shape=jax.ShapeDtypeStruct((rows, n), F32),
        compiler_params=_params(("arbitrary",), 32 * MIB),
        name="ada",
    )(c_pad, w_ada, b_ada)


MOD_TR = 512


def _modulate_kernel(x_ref, sc_ref, sh_ref, o_ref):
    o_ref[...] = (x_ref[...] * (1.0 + sc_ref[0]) + sh_ref[0]).astype(BF16)


def _modulate(x2, sc, sh, seq):
    m, d = x2.shape
    per_batch = seq // MOD_TR
    vec = pl.BlockSpec((1, 1, d), lambda i: (i // per_batch, 0, 0))
    return pl.pallas_call(
        _modulate_kernel,
        grid=(m // MOD_TR,),
        in_specs=[pl.BlockSpec((MOD_TR, d), lambda i: (i, 0)), vec, vec],
        out_specs=pl.BlockSpec((MOD_TR, d), lambda i: (i, 0)),
        out_shape=jax.ShapeDtypeStruct((m, d), BF16),
        compiler_params=_params(("arbitrary",), 40 * MIB),
        name="modulate",
    )(x2, sc, sh)


ROPE_TR = 1024


def _rope_kernel(pos_ref, inv_ref, cos_ref, sin_ref):
    ang = pos_ref[...] * inv_ref[...]
    cos_ref[...] = jnp.cos(ang)
    sin_ref[...] = jnp.sin(ang)


def _rope_tables(pos_col, inv_freq):
    m = pos_col.shape[0]
    half = inv_freq.shape[1]
    out = jax.ShapeDtypeStruct((m, half), F32)
    return pl.pallas_call(
        _rope_kernel,
        grid=(m // ROPE_TR,),
        in_specs=[pl.BlockSpec((ROPE_TR, 1), lambda i: (i, 0)),
                  pl.BlockSpec((1, half), lambda i: (0, 0))],
        out_specs=[pl.BlockSpec((ROPE_TR, half), lambda i: (i, 0))] * 2,
        out_shape=[out, out],
        compiler_params=_params(("arbitrary",), 32 * MIB),
        name="rope",
    )(pos_col, inv_freq)


Z_TM = 1024
Z_TN = 1024
Z_RC = 256


def _zproj_kernel(*refs, nt, mt, bounds, hosted):
    nh = len(hosted)
    h_ref, w_ref, cos_ref, sin_ref = refs[:4]
    hosted_src = refs[4:4 + nh]
    o_ref = refs[4 + nh]
    hosted_dst = refs[5 + nh:5 + 2 * nh]
    wcur_ref, wnext_ref = refs[5 + 2 * nh:7 + 2 * nh]
    n = pl.program_id(0)
    m = pl.program_id(1)
    half = HEAD_DIM // 2
    kc = w_ref.shape[0]

    def stage_and_host():
        wnext_ref[pl.ds(pl.multiple_of(m * kc, kc), kc), :] = w_ref[...].astype(BF16)
        for cast, src, dst in zip(hosted, hosted_src, hosted_dst):
            cast.run(src, dst)

    @pl.when(n == 0)
    def _no_tile_yet():
        stage_and_host()

    t = n - 1

    def run(epilogue):
        stage_and_host()
        for r in range(Z_TM // Z_RC):
            rows = slice(r * Z_RC, (r + 1) * Z_RC)
            acc = jnp.dot(h_ref[rows, :], wcur_ref[...], preferred_element_type=F32)
            epilogue(acc, rows)

    def rotary(acc, rows):
        scale = jnp.where(t < q_end, 1.0, HEAD_DIM ** -0.5).astype(F32)
        c = cos_ref[rows, :]
        s = sin_ref[rows, :]
        for j in range(Z_TN // HEAD_DIM):
            x1 = acc[:, j * HEAD_DIM: j * HEAD_DIM + half]
            x2 = acc[:, j * HEAD_DIM + half: (j + 1) * HEAD_DIM]
            o_ref[rows, j * HEAD_DIM: j * HEAD_DIM + half] = ((x1 * c - x2 * s) * scale).astype(BF16)
            o_ref[rows, j * HEAD_DIM + half: (j + 1) * HEAD_DIM] = ((x2 * c + x1 * s) * scale).astype(BF16)

    def pointwise(fn):
        def epi(acc, rows):
            o_ref[rows, :] = fn(acc).astype(BF16)
        return epi

    q_end, k_end, v_end, g_end, us_end = bounds
    pl.when((n >= 1) & (t < k_end))(lambda: run(rotary))
    pl.when((t >= k_end) & (t < v_end))(lambda: run(pointwise(lambda a: a)))
    pl.when((t >= v_end) & (t < g_end))(lambda: run(pointwise(jax.nn.silu)))
    pl.when((t >= g_end) & (t < us_end))(lambda: run(pointwise(jax.nn.gelu)))
    pl.when(t >= us_end)(lambda: run(pointwise(jax.nn.sigmoid)))

    @pl.when(m == mt - 1)
    def _sweep_done():
        wcur_ref[...] = wnext_ref[...]


def _zproj(h1, w_in, cos, sin, hosted_weights):
    m, d = h1.shape
    n = w_in.shape[1]
    nt, mt = n // Z_TN, m // Z_TM
    kc = d // mt
    ret_w = RET_HEADS * HEAD_DIM
    sg_w = SG_GROUPS * SG_GROUP_DIM
    bounds = tuple(b // Z_TN for b in (ret_w, 2 * ret_w, 3 * ret_w, 4 * ret_w, 4 * ret_w + 2 * sg_w))
    half = cos.shape[1]
    hosted = [_HostedCast(w, rb, first, lambda j, i: j * mt + i) for w, rb, first in hosted_weights]
    assert all(c.first_step + c.num_blocks <= (nt + 1) * mt for c in hosted)

    def row_tile(j, i):
        return (jnp.where(j == 0, 0, i), 0)

    def out_tile(j, i):
        return (jnp.where(j == 0, 0, i), jnp.maximum(j - 1, 0))

    outs = pl.pallas_call(
        functools.partial(_zproj_kernel, nt=nt, mt=mt, bounds=bounds, hosted=hosted),
        grid=(nt + 1, mt),
        in_specs=[pl.BlockSpec((Z_TM, d), row_tile),
                  pl.BlockSpec((kc, Z_TN), lambda j, i: (i, jnp.minimum(j, nt - 1))),
                  pl.BlockSpec((Z_TM, half), row_tile),
                  pl.BlockSpec((Z_TM, half), row_tile)] + [c.spec for c in hosted],
        out_specs=[pl.BlockSpec((Z_TM, Z_TN), out_tile)]
                  + [c.spec for c in hosted],
        out_shape=[jax.ShapeDtypeStruct((m, n), BF16)] + [c.out_shape for c in hosted],
        scratch_shapes=[pltpu.VMEM((d, Z_TN), BF16), pltpu.VMEM((d, Z_TN), BF16)],
        compiler_params=_params(("arbitrary", "arbitrary"), V7X_VMEM_BUDGET),
        name="zproj",
    )(h1, w_in, cos, sin, *[c.weight for c in hosted])
    return outs[0], outs[1:]


def _retention_kernel(cd_ref, q_ref, k_ref, v_ref, g_ref, dec_ref, xi_ref, zeta_ref, o_ref, state_ref):
    @pl.when(pl.program_id(1) == 0)
    def _():
        state_ref[...] = jnp.zeros_like(state_ref)

    for h in range(RET_HEADS):
        cols = slice(h * HEAD_DIM, (h + 1) * HEAD_DIM)
        q = q_ref[:, cols]
        k = k_ref[:, cols]
        v = v_ref[:, cols]
        scores = lax.dot_general(q, k, (((1,), (1,)), ((), ())), preferred_element_type=F32)
        scores = scores * dec_ref[h]
        inner = jnp.dot(scores.astype(BF16), v, preferred_element_type=F32)
        state = state_ref[h]
        cross = jnp.dot(q, state.astype(BF16), preferred_element_type=F32) * xi_ref[h]
        kz = (k.astype(F32) * zeta_ref[h]).astype(BF16)
        update = lax.dot_general(kz, v, (((0,), (0,)), ((), ())), preferred_element_type=F32)
        state_ref[h] = cd_ref[h] * state + update
        y = inner + cross
        mu = jnp.mean(y, axis=-1, keepdims=True)
        yc = y - mu
        var = jnp.mean(yc * yc, axis=-1, keepdims=True)
        yn = yc * lax.rsqrt(var + LN_EPS)
        o_ref[:, cols] = (yn * g_ref[:, cols].astype(F32)).astype(BF16)


def _sgu_kernel(u_ref, s_ref, lng_ref, lnb_ref, w_ref, b_ref, o_ref):
    s = s_ref[...].astype(F32)
    mu = jnp.mean(s, axis=-1, keepdims=True)
    sc = s - mu
    var = jnp.mean(sc * sc, axis=-1, keepdims=True)
    sn = (sc * lax.rsqrt(var + LN_EPS) * lng_ref[...] + lnb_ref[...]).astype(BF16)
    t_idx = lax.broadcasted_iota(jnp.int32, (CHUNK, CHUNK), 0)
    s_idx = lax.broadcasted_iota(jnp.int32, (CHUNK, CHUNK), 1)
    causal = t_idx >= s_idx
    for g in range(SG_GROUPS):
        cols = slice(g * SG_GROUP_DIM, (g + 1) * SG_GROUP_DIM)
        w = jnp.where(causal, w_ref[g], 0.0).astype(BF16)
        mixed = jnp.dot(w, sn[:, cols], preferred_element_type=F32) + b_ref[g]
        o_ref[:, cols] = (u_ref[:, cols].astype(F32) * mixed).astype(BF16)


def _mixers_kernel(cd_ref, q_ref, k_ref, v_ref, g_ref, dec_ref, xi_ref, zeta_ref,
                   u_ref, s_ref, lng_ref, lnb_ref, w_ref, b_ref, ret_ref, sgu_ref, state_ref):
    _retention_kernel(cd_ref, q_ref, k_ref, v_ref, g_ref, dec_ref, xi_ref, zeta_ref, ret_ref, state_ref)
    _sgu_kernel(u_ref, s_ref, lng_ref, lnb_ref, w_ref, b_ref, sgu_ref)


def _mixers(z, chunk_decay, inner_decay, xi_b, zeta_b, ln_g, ln_b, w_s, b_col, batch, seq):
    m = z.shape[0]
    width = RET_HEADS * HEAD_DIM
    assert width == SG_GROUPS * SG_GROUP_DIM
    n_chunks = seq // CHUNK

    def zblock(col):
        return pl.BlockSpec((CHUNK, width), lambda b, c: (b * n_chunks + c, col))

    def const(shape):
        return pl.BlockSpec(shape, lambda b, c: (0,) * len(shape))

    out_spec = pl.BlockSpec((CHUNK, width), lambda b, c: (b * n_chunks + c, 0))
    out_shape = jax.ShapeDtypeStruct((m, width), BF16)
    return pl.pallas_call(
        _mixers_kernel,
        grid=(batch, n_chunks),
        in_specs=[pl.BlockSpec(memory_space=pltpu.SMEM),
                  zblock(0), zblock(1), zblock(2), zblock(3),
                  const(inner_decay.shape), const(xi_b.shape), const(zeta_b.shape),
                  zblock(4), zblock(5), const(ln_g.shape), const(ln_b.shape),
                  const(w_s.shape), const(b_col.shape)],
        out_specs=[out_spec, out_spec],
        out_shape=[out_shape, out_shape],
        scratch_shapes=[pltpu.VMEM((RET_HEADS, HEAD_DIM, HEAD_DIM), F32)],
        compiler_params=_params(("arbitrary", "arbitrary"), 32 * MIB),
        name="mixers",
    )(chunk_decay, z, z, z, z, inner_decay, xi_b, zeta_b, z, z, ln_g, ln_b, w_s, b_col)


MIX_TM = 1024
MIX_TN = 1024
MIX_RC = 256


def _mix_kernel(ret_ref, sgu_ref, wr_ref, ws_ref, gr_ref, gs_ref, o_ref):
    for r in range(MIX_TM // MIX_RC):
        rows = slice(r * MIX_RC, (r + 1) * MIX_RC)
        y_ret = jnp.dot(ret_ref[rows, :], wr_ref[...], preferred_element_type=F32)
        y_sg = jnp.dot(sgu_ref[rows, :], ws_ref[...], preferred_element_type=F32)
        mix = gr_ref[rows, :].astype(F32) * y_ret + gs_ref[rows, :].astype(F32) * y_sg
        o_ref[rows, :] = mix.astype(BF16)


def _mix(ret, sgu, w_ret_up, w_sg_up, z, gate_col0):
    m, kdim = ret.shape
    n = w_ret_up.shape[1]
    nt = n // MIX_TN
    g0 = gate_col0 // MIX_TN
    g1 = g0 + nt
    return pl.pallas_call(
        _mix_kernel,
        grid=(m // MIX_TM, nt),
        in_specs=[pl.BlockSpec((MIX_TM, kdim), lambda i, j: (i, 0)),
                  pl.BlockSpec((MIX_TM, kdim), lambda i, j: (i, 0)),
                  pl.BlockSpec((kdim, MIX_TN), lambda i, j: (0, j)),
                  pl.BlockSpec((kdim, MIX_TN), lambda i, j: (0, j)),
                  pl.BlockSpec((MIX_TM, MIX_TN), lambda i, j: (i, g0 + j)),
                  pl.BlockSpec((MIX_TM, MIX_TN), lambda i, j: (i, g1 + j))],
        out_specs=pl.BlockSpec((MIX_TM, MIX_TN), lambda i, j: (i, j)),
        out_shape=jax.ShapeDtypeStruct((m, n), BF16),
        compiler_params=_params(("arbitrary", "arbitrary"), V7X_VMEM_BUDGET),
        name="mix",
    )(ret, sgu, w_ret_up, w_sg_up, z, z)


OUT_TM = 1024
OUT_TN = 1024
OUT_RC = 256


def _outproj_kernel(a_ref, w_ref, o_ref):
    for r in range(OUT_TM // OUT_RC):
        rows = slice(r * OUT_RC, (r + 1) * OUT_RC)
        o_ref[rows, :] = jnp.dot(a_ref[rows, :], w_ref[...], preferred_element_type=F32).astype(BF16)


def _outproj(a, w):
    m, kdim = a.shape
    n = w.shape[1]
    return pl.pallas_call(
        _outproj_kernel,
        grid=(m // OUT_TM, n // OUT_TN),
        in_specs=[pl.BlockSpec((OUT_TM, kdim), lambda i, j: (i, 0)),
                  pl.BlockSpec((kdim, OUT_TN), lambda i, j: (0, j))],
        out_specs=pl.BlockSpec((OUT_TM, OUT_TN), lambda i, j: (i, j)),
        out_shape=jax.ShapeDtypeStruct((m, n), BF16),
        compiler_params=_params(("arbitrary", "arbitrary"), 48 * MIB),
        name="outproj",
    )(a, w)


LN_TR = 256


def _residual_ln(x, branch, gate, ln_g, ln_b):
    y = ALPHA * x + (1.0 + gate) * branch.astype(F32)
    mu = jnp.mean(y, axis=-1, keepdims=True)
    yc = y - mu
    var = jnp.mean(yc * yc, axis=-1, keepdims=True)
    return yc * lax.rsqrt(var + LN_EPS) * ln_g + ln_b


def _ln1_kernel(x_ref, br_ref, gate_ref, lng_ref, lnb_ref, sc_ref, sh_ref, x1_ref, h2_ref):
    x1 = _residual_ln(x_ref[...], br_ref[...], gate_ref[0], lng_ref[...], lnb_ref[...])
    x1_ref[...] = x1
    h2_ref[...] = (x1 * (1.0 + sc_ref[0]) + sh_ref[0]).astype(BF16)


def _ln2_kernel(x_ref, br_ref, gate_ref, lng_ref, lnb_ref, o_ref):
    o_ref[...] = _residual_ln(x_ref[...], br_ref[...], gate_ref[0], lng_ref[...], lnb_ref[...])


def _ln_call(kernel, name, x, branch, batch_vecs, row_vecs, out_dtypes, seq):
    m, d = x.shape
    per_batch = seq // LN_TR
    tile = pl.BlockSpec((LN_TR, d), lambda i: (i, 0))
    bvec = pl.BlockSpec((1, 1, d), lambda i: (i // per_batch, 0, 0))
    rvec = pl.BlockSpec((1, d), lambda i: (0, 0))
    operands = [x, branch, batch_vecs[0], *row_vecs, *batch_vecs[1:]]
    specs = [tile, tile, bvec] + [rvec] * len(row_vecs) + [bvec] * (len(batch_vecs) - 1)
    return pl.pallas_call(
        kernel,
        grid=(m // LN_TR,),
        in_specs=specs,
        out_specs=[tile] * len(out_dtypes),
        out_shape=[jax.ShapeDtypeStruct((m, d), dt) for dt in out_dtypes],
        compiler_params=_params(("arbitrary",), 48 * MIB),
        name=name,
    )(*operands)


UP_TM = 4096
UP_TF = 256
UP_RC = 512
DOWN_TM = 512
DOWN_TN = 1024
DOWN_RC = 256
RESIDENT = pl.Buffered(1)


def _ffn_up_kernel(h_ref, wg_ref, wu_ref, cast_src, o_ref, cast_dst):
    _HostedCast.run(cast_src, cast_dst)
    for r in range(UP_TM // UP_RC):
        rows = slice(r * UP_RC, (r + 1) * UP_RC)
        h = h_ref[rows, :]
        a_gate = jnp.dot(h, wg_ref[...], preferred_element_type=F32)
        a_up = jnp.dot(h, wu_ref[...], preferred_element_type=F32)
        o_ref[rows, :] = (jax.nn.silu(a_gate) * a_up).astype(BF16)


def _ffn_up(h2, w_gate_up, hosted_weight, hosted_rows):
    m, d = h2.shape
    d_ff = w_gate_up.shape[1] // 2
    nf = d_ff // UP_TF
    hosted = _HostedCast(hosted_weight, hosted_rows, 0, lambda i, f: i * nf + f)
    assert hosted.num_blocks <= (m // UP_TM) * nf
    return pl.pallas_call(
        _ffn_up_kernel,
        grid=(m // UP_TM, nf),
        in_specs=[pl.BlockSpec((UP_TM, d), lambda i, f: (i, 0), pipeline_mode=RESIDENT),
                  pl.BlockSpec((d, UP_TF), lambda i, f: (0, f)),
                  pl.BlockSpec((d, UP_TF), lambda i, f: (0, nf + f)),
                  hosted.spec],
        out_specs=[pl.BlockSpec((UP_TM, UP_TF), lambda i, f: (i, f)), hosted.spec],
        out_shape=[jax.ShapeDtypeStruct((m, d_ff), BF16), hosted.out_shape],
        compiler_params=_params(("arbitrary", "arbitrary"), V7X_VMEM_BUDGET),
        name="ffn_up",
    )(h2, w_gate_up, w_gate_up, hosted_weight)


def _ffn_down_kernel(a_ref, w_ref, o_ref):
    for r in range(DOWN_TM // DOWN_RC):
        rows = slice(r * DOWN_RC, (r + 1) * DOWN_RC)
        o_ref[rows, :] = jnp.dot(a_ref[rows, :], w_ref[...], preferred_element_type=F32).astype(BF16)


def _ffn_down(act, w_down):
    m, d_ff = act.shape
    d = w_down.shape[1]
    return pl.pallas_call(
        _ffn_down_kernel,
        grid=(d // DOWN_TN, m // DOWN_TM),
        in_specs=[pl.BlockSpec((DOWN_TM, d_ff), lambda j, i: (i, 0)),
                  pl.BlockSpec((d_ff, DOWN_TN), lambda j, i: (0, j), pipeline_mode=RESIDENT)],
        out_specs=pl.BlockSpec((DOWN_TM, DOWN_TN), lambda j, i: (i, j)),
        out_shape=jax.ShapeDtypeStruct((m, d), BF16),
        compiler_params=_params(("arbitrary", "arbitrary"), V7X_VMEM_BUDGET),
        name="ffn_down",
    )(act, w_down)


def _decay_tables():
    log_g = jnp.log(1.0 - 2.0 ** (-5.0 - jnp.arange(RET_HEADS, dtype=F32)))
    idx = jnp.arange(CHUNK, dtype=F32)
    diff = idx[:, None] - idx[None, :]
    inner = jnp.where((diff >= 0)[None],
                      jnp.exp(log_g[:, None, None] * jnp.maximum(diff, 0.0)[None]), 0.0)
    xi = jnp.exp(log_g[:, None] * (idx[None, :] + 1.0))
    zeta = jnp.exp(log_g[:, None] * (CHUNK - 1.0 - idx[None, :]))
    chunk_decay = jnp.exp(log_g * CHUNK)
    xi_b = jnp.broadcast_to(xi[:, :, None], (RET_HEADS, CHUNK, HEAD_DIM))
    zeta_b = jnp.broadcast_to(zeta[:, :, None], (RET_HEADS, CHUNK, HEAD_DIM))
    return chunk_decay, inner, xi_b, zeta_b


def kernel(x, c, positions, w_ada, b_ada, w_in, sg_ln_g, sg_ln_b, sg_w, sg_b, w_ret_up, w_sg_up,
           w_out, ln1_g, ln1_b, w_gate_up, w_down, ln2_g, ln2_b):
    batch, seq, d = x.shape
    m = batch * seq
    assert w_ada.shape[0] == DEPTH == 1
    x2 = x.reshape(m, d)

    c_pad = jnp.zeros((8, d), F32).at[:batch].set(c)
    ada = _ada(c_pad, w_ada[0], b_ada[0][None, :])[:batch]
    sh1, sc1, g1, sh2, sc2, g2 = [t[:, None, :] for t in jnp.split(ada, 6, axis=-1)]

    h1 = _modulate(x2, sc1, sh1, seq)

    inv_freq = (ROPE_BASE ** (-jnp.arange(0, HEAD_DIM, 2, dtype=F32) / HEAD_DIM))[None, :]
    cos, sin = _rope_tables(positions.astype(F32).reshape(m, 1), inv_freq)

    z, (w_gate_up_bf, w_out_bf, w_ret_up_bf, w_sg_up_bf) = _zproj(
        h1, w_in[0], cos, sin,
        [(w_gate_up[0], 32, 0), (w_out[0], 32, 0), (w_ret_up[0], 16, 40), (w_sg_up[0], 16, 40)])

    chunk_decay, inner_decay, xi_b, zeta_b = _decay_tables()
    ret, sgu = _mixers(z, chunk_decay, inner_decay, xi_b, zeta_b, sg_ln_g[0][None, :],
                       sg_ln_b[0][None, :], sg_w[0], sg_b[0][:, :, None], batch, seq)

    gate_col0 = 4 * RET_HEADS * HEAD_DIM + 2 * SG_GROUPS * SG_GROUP_DIM
    mix = _mix(ret, sgu, w_ret_up_bf, w_sg_up_bf, z, gate_col0)
    branch1 = _outproj(mix, w_out_bf)
    x1, h2 = _ln_call(_ln1_kernel, "ln1", x2, branch1, [g1, sc2, sh2],
                      [ln1_g[0][None, :], ln1_b[0][None, :]], [F32, BF16], seq)

    act, w_down_bf = _ffn_up(h2, w_gate_up_bf, w_down[0], 128)
    ffn = _ffn_down(act, w_down_bf)
    (out,) = _ln_call(_ln2_kernel, "ln2", x1, ffn, [g2],
                      [ln2_g[0][None, :], ln2_b[0][None, :]], [F32], seq)
    return out.reshape(batch, seq, d)
```

```python
import functools

import jax
import jax.numpy as jnp
from jax import lax
from jax.experimental import pallas as pl
from jax.experimental.pallas import tpu as pltpu

F32 = jnp.float32
BF16 = jnp.bfloat16

RET_HEADS = 8
HEAD_DIM = 256
SG_GROUPS = 8
SG_GROUP_DIM = 256
CHUNK = 128
ROPE_BASE = 10000.0
LN_EPS = 1e-5
DEPTH = 1
ALPHA = (2.0 * DEPTH) ** 0.25

MIB = 1024 * 1024
V7X_VMEM_BUDGET = 56 * MIB


def _params(sem, vmem_bytes):
    return pltpu.CompilerParams(dimension_semantics=sem, vmem_limit_bytes=vmem_bytes)


class _HostedCast:
    def __init__(self, weight, rows_per_block, first_step, step_of):
        rows, cols = weight.shape
        self.weight = weight
        self.first_step = first_step
        self.num_blocks = rows // rows_per_block
        assert self.num_blocks * rows_per_block == rows
        last = self.num_blocks - 1

        def index_map(*grid_ids):
            return (jnp.clip(step_of(*grid_ids) - first_step, 0, last), 0)

        self.spec = pl.BlockSpec((rows_per_block, cols), index_map)
        self.out_shape = jax.ShapeDtypeStruct((rows, cols), BF16)

    @staticmethod
    def run(src_ref, dst_ref):
        dst_ref[...] = src_ref[...].astype(BF16)


ADA_TN = 512


def _ada_kernel(c_ref, w_ref, b_ref, o_ref):
    s = jax.nn.silu(c_ref[...]).astype(BF16)
    o_ref[...] = jnp.dot(s, w_ref[...].astype(BF16), preferred_element_type=F32) + b_ref[...]


def _ada(c_pad, w_ada, b_ada):
    rows, d = c_pad.shape
    n = w_ada.shape[1]
    return pl.pallas_call(
        _ada_kernel,
        grid=(n // ADA_TN,),
        in_specs=[pl.BlockSpec((rows, d), lambda j: (0, 0)),
                  pl.BlockSpec((d, ADA_TN), lambda j: (0, j)),
                  pl.BlockSpec((1, ADA_TN), lambda j: (0, j))],
        out_specs=pl.BlockSpec((rows, ADA_TN), lambda j: (0, j)),
        out_shape=jax.ShapeDtypeStruct((rows, n), F32),
        compiler_params=_params(("arbitrary",), 32 * MIB),
        name="ada",
    )(c_pad, w_ada, b_ada)


MOD_TR = 512


def _modulate_kernel(x_ref, sc_ref, sh_ref, o_ref):
    o_ref[...] = (x_ref[...] * (1.0 + sc_ref[0]) + sh_ref[0]).astype(BF16)


def _modulate(x2, sc, sh, seq):
    m, d = x2.shape
    per_batch = seq // MOD_TR
    vec = pl.BlockSpec((1, 1, d), lambda i: (i // per_batch, 0, 0))
    return pl.pallas_call(
        _modulate_kernel,
        grid=(m // MOD_TR,),
        in_specs=[pl.BlockSpec((MOD_TR, d), lambda i: (i, 0)), vec, vec],
        out_specs=pl.BlockSpec((MOD_TR, d), lambda i: (i, 0)),
        out_shape=jax.ShapeDtypeStruct((m, d), BF16),
        compiler_params=_params(("arbitrary",), 40 * MIB),
        name="modulate",
    )(x2, sc, sh)


ROPE_TR = 1024


def _rope_kernel(pos_ref, inv_ref, cos_ref, sin_ref):
    ang = pos_ref[...] * inv_ref[...]
    cos_ref[...] = jnp.cos(ang)
    sin_ref[...] = jnp.sin(ang)


def _rope_tables(pos_col, inv_freq):
    m = pos_col.shape[0]
    half = inv_freq.shape[1]
    out = jax.ShapeDtypeStruct((m, half), F32)
    return pl.pallas_call(
        _rope_kernel,
        grid=(m // ROPE_TR,),
        in_specs=[pl.BlockSpec((ROPE_TR, 1), lambda i: (i, 0)),
                  pl.BlockSpec((1, half), lambda i: (0, 0))],
        out_specs=[pl.BlockSpec((ROPE_TR, half), lambda i: (i, 0))] * 2,
        out_shape=[out, out],
        compiler_params=_params(("arbitrary",), 32 * MIB),
        name="rope",
    )(pos_col, inv_freq)


Z_TM = 1024
Z_TN = 1024
Z_RC = 256


def _zproj_kernel(*refs, nt, mt, bounds, hosted):
    nh = len(hosted)
    h_ref, w_ref, cos_ref, sin_ref = refs[:4]
    hosted_src = refs[4:4 + nh]
    o_ref = refs[4 + nh]
    hosted_dst = refs[5 + nh:5 + 2 * nh]
    wcur_ref, wnext_ref = refs[5 + 2 * nh:7 + 2 * nh]
    n = pl.program_id(0)
    m = pl.program_id(1)
    half = HEAD_DIM // 2
    kc = w_ref.shape[0]

    def stage_and_host():
        wnext_ref[pl.ds(pl.multiple_of(m * kc, kc), kc), :] = w_ref[...].astype(BF16)
        for cast, src, dst in zip(hosted, hosted_src, hosted_dst):
            cast.run(src, dst)

    @pl.when(n == 0)
    def _no_tile_yet():
        stage_and_host()

    t = n - 1

    def run(epilogue):
        stage_and_host()
        for r in range(Z_TM // Z_RC):
            rows = slice(r * Z_RC, (r + 1) * Z_RC)
            acc = jnp.dot(h_ref[rows, :], wcur_ref[...], preferred_element_type=F32)
            epilogue(acc, rows)

    def rotary(acc, rows):
        scale = jnp.where(t < q_end, 1.0, HEAD_DIM ** -0.5).astype(F32)
        c = cos_ref[rows, :]
        s = sin_ref[rows, :]
        for j in range(Z_TN // HEAD_DIM):
            x1 = acc[:, j * HEAD_DIM: j * HEAD_DIM + half]
            x2 = acc[:, j * HEAD_DIM + half: (j + 1) * HEAD_DIM]
            o_ref[rows, j * HEAD_DIM: j * HEAD_DIM + half] = ((x1 * c - x2 * s) * scale).astype(BF16)
            o_ref[rows, j * HEAD_DIM + half: (j + 1) * HEAD_DIM] = ((x2 * c + x1 * s) * scale).astype(BF16)

    def pointwise(fn):
        def epi(acc, rows):
            o_ref[rows, :] = fn(acc).astype(BF16)
        return epi

    q_end, k_end, v_end, g_end, us_end = bounds
    pl.when((n >= 1) & (t < k_end))(lambda: run(rotary))
    pl.when((t >= k_end) & (t < g_end))(
        lambda: run(pointwise(lambda a: jnp.where(t < v_end, a, jax.nn.silu(a)))))
    pl.when((t >= g_end) & (t < us_end))(lambda: run(pointwise(jax.nn.gelu)))
    pl.when(t >= us_end)(lambda: run(pointwise(jax.nn.sigmoid)))

    @pl.when(m == mt - 1)
    def _sweep_done():
        wcur_ref[...] = wnext_ref[...]


def _zproj(h1, w_in, cos, sin, hosted_weights):
    m, d = h1.shape
    n = w_in.shape[1]
    nt, mt = n // Z_TN, m // Z_TM
    kc = d // mt
    ret_w = RET_HEADS * HEAD_DIM
    sg_w = SG_GROUPS * SG_GROUP_DIM
    bounds = tuple(b // Z_TN for b in (ret_w, 2 * ret_w, 3 * ret_w, 4 * ret_w, 4 * ret_w + 2 * sg_w))
    half = cos.shape[1]
    hosted = [_HostedCast(w, rb, first, lambda j, i: j * mt + i) for w, rb, first in hosted_weights]
    assert all(c.first_step + c.num_blocks <= (nt + 1) * mt for c in hosted)

    def row_tile(j, i):
        return (jnp.where(j == 0, 0, i), 0)

    def out_tile(j, i):
        return (jnp.where(j == 0, 0, i), jnp.maximum(j - 1, 0))

    outs = pl.pallas_call(
        functools.partial(_zproj_kernel, nt=nt, mt=mt, bounds=bounds, hosted=hosted),
        grid=(nt + 1, mt),
        in_specs=[pl.BlockSpec((Z_TM, d), row_tile),
                  pl.BlockSpec((kc, Z_TN), lambda j, i: (i, jnp.minimum(j, nt - 1))),
                  pl.BlockSpec((Z_TM, half), row_tile),
                  pl.BlockSpec((Z_TM, half), row_tile)] + [c.spec for c in hosted],
        out_specs=[pl.BlockSpec((Z_TM, Z_TN), out_tile)]
                  + [c.spec for c in hosted],
        out_shape=[jax.ShapeDtypeStruct((m, n), BF16)] + [c.out_shape for c in hosted],
        scratch_shapes=[pltpu.VMEM((d, Z_TN), BF16), pltpu.VMEM((d, Z_TN), BF16)],
        compiler_params=_params(("arbitrary", "arbitrary"), V7X_VMEM_BUDGET),
        name="zproj",
    )(h1, w_in, cos, sin, *[c.weight for c in hosted])
    return outs[0], outs[1:]


def _retention_kernel(cd_ref, q_ref, k_ref, v_ref, g_ref, dec_ref, xi_ref, zeta_ref, o_ref, state_ref):
    @pl.when(pl.program_id(1) == 0)
    def _():
        state_ref[...] = jnp.zeros_like(state_ref)

    for h in range(RET_HEADS):
        cols = slice(h * HEAD_DIM, (h + 1) * HEAD_DIM)
        q = q_ref[:, cols]
        k = k_ref[:, cols]
        v = v_ref[:, cols]
        scores = lax.dot_general(q, k, (((1,), (1,)), ((), ())), preferred_element_type=F32)
        scores = scores * dec_ref[h]
        inner = jnp.dot(scores.astype(BF16), v, preferred_element_type=F32)
        state = state_ref[h]
        cross = jnp.dot(q, state.astype(BF16), preferred_element_type=F32) * xi_ref[h]
        kz = (k.astype(F32) * zeta_ref[h]).astype(BF16)
        update = lax.dot_general(kz, v, (((0,), (0,)), ((), ())), preferred_element_type=F32)
        state_ref[h] = cd_ref[h] * state + update
        y = inner + cross
        mu = jnp.mean(y, axis=-1, keepdims=True)
        yc = y - mu
        var = jnp.mean(yc * yc, axis=-1, keepdims=True)
        yn = yc * lax.rsqrt(var + LN_EPS)
        o_ref[:, cols] = (yn * g_ref[:, cols].astype(F32)).astype(BF16)


def _sgu_kernel(u_ref, s_ref, lng_ref, lnb_ref, w_ref, b_ref, o_ref):
    s = s_ref[...].astype(F32)
    mu = jnp.mean(s, axis=-1, keepdims=True)
    sc = s - mu
    var = jnp.mean(sc * sc, axis=-1, keepdims=True)
    sn = (sc * lax.rsqrt(var + LN_EPS) * lng_ref[...] + lnb_ref[...]).astype(BF16)
    t_idx = lax.broadcasted_iota(jnp.int32, (CHUNK, CHUNK), 0)
    s_idx = lax.broadcasted_iota(jnp.int32, (CHUNK, CHUNK), 1)
    causal = t_idx >= s_idx
    for g in range(SG_GROUPS):
        cols = slice(g * SG_GROUP_DIM, (g + 1) * SG_GROUP_DIM)
        w = jnp.where(causal, w_ref[g], 0.0).astype(BF16)
        mixed = jnp.dot(w, sn[:, cols], preferred_element_type=F32) + b_ref[g]
        o_ref[:, cols] = (u_ref[:, cols].astype(F32) * mixed).astype(BF16)


def _mixers_kernel(cd_ref, q_ref, k_ref, v_ref, g_ref, dec_ref, xi_ref, zeta_ref,
                   u_ref, s_ref, lng_ref, lnb_ref, w_ref, b_ref, ret_ref, sgu_ref, state_ref):
    _retention_kernel(cd_ref, q_ref, k_ref, v_ref, g_ref, dec_ref, xi_ref, zeta_ref, ret_ref, state_ref)
    _sgu_kernel(u_ref, s_ref, lng_ref, lnb_ref, w_ref, b_ref, sgu_ref)


def _mixers(z, chunk_decay, inner_decay, xi_b, zeta_b, ln_g, ln_b, w_s, b_col, batch, seq):
    m = z.shape[0]
    width = RET_HEADS * HEAD_DIM
    assert width == SG_GROUPS * SG_GROUP_DIM
    n_chunks = seq // CHUNK

    def zblock(col):
        return pl.BlockSpec((CHUNK, width), lambda b, c: (b * n_chunks + c, col))

    def const(shape):
        return pl.BlockSpec(shape, lambda b, c: (0,) * len(shape))

    out_spec = pl.BlockSpec((CHUNK, width), lambda b, c: (b * n_chunks + c, 0))
    out_shape = jax.ShapeDtypeStruct((m, width), BF16)
    return pl.pallas_call(
        _mixers_kernel,
        grid=(batch, n_chunks),
        in_specs=[pl.BlockSpec(memory_space=pltpu.SMEM),
                  zblock(0), zblock(1), zblock(2), zblock(3),
                  const(inner_decay.shape), const(xi_b.shape), const(zeta_b.shape),
                  zblock(4), zblock(5), const(ln_g.shape), const(ln_b.shape),
                  const(w_s.shape), const(b_col.shape)],
        out_specs=[out_spec, out_spec],
        out_shape=[out_shape, out_shape],
        scratch_shapes=[pltpu.VMEM((RET_HEADS, HEAD_DIM, HEAD_DIM), F32)],
        compiler_params=_params(("arbitrary", "arbitrary"), 32 * MIB),
        name="mixers",
    )(chunk_decay, z, z, z, z, inner_decay, xi_b, zeta_b, z, z, ln_g, ln_b, w_s, b_col)


MIX_TM = 1024
MIX_TN = 1024
MIX_RC = 256


def _mix_kernel(ret_ref, sgu_ref, wr_ref, ws_ref, gr_ref, gs_ref, o_ref):
    for r in range(MIX_TM // MIX_RC):
        rows = slice(r * MIX_RC, (r + 1) * MIX_RC)
        y_ret = jnp.dot(ret_ref[rows, :], wr_ref[...], preferred_element_type=F32)
        y_sg = jnp.dot(sgu_ref[rows, :], ws_ref[...], preferred_element_type=F32)
        mix = gr_ref[rows, :].astype(F32) * y_ret + gs_ref[rows, :].astype(F32) * y_sg
        o_ref[rows, :] = mix.astype(BF16)


def _mix(ret, sgu, w_ret_up, w_sg_up, z, gate_col0):
    m, kdim = ret.shape
    n = w_ret_up.shape[1]
    nt = n // MIX_TN
    g0 = gate_col0 // MIX_TN
    g1 = g0 + nt
    return pl.pallas_call(
        _mix_kernel,
        grid=(m // MIX_TM, nt),
        in_specs=[pl.BlockSpec((MIX_TM, kdim), lambda i, j: (i, 0)),
                  pl.BlockSpec((MIX_TM, kdim), lambda i, j: (i, 0)),
                  pl.BlockSpec((kdim, MIX_TN), lambda i, j: (0, j)),
                  pl.BlockSpec((kdim, MIX_TN), lambda i, j: (0, j)),
                  pl.BlockSpec((MIX_TM, MIX_TN), lambda i, j: (i, g0 + j)),
                  pl.BlockSpec((MIX_TM, MIX_TN), lambda i, j: (i, g1 + j))],
        out_specs=pl.BlockSpec((MIX_TM, MIX_TN), lambda i, j: (i, j)),
        out_shape=jax.ShapeDtypeStruct((m, n), BF16),
        compiler_params=_params(("arbitrary", "arbitrary"), V7X_VMEM_BUDGET),
        name="mix",
    )(ret, sgu, w_ret_up, w_sg_up, z, z)


OUT_TM = 1024
OUT_TN = 1024
OUT_RC = 256


def _outproj_kernel(a_ref, w_ref, o_ref):
    for r in range(OUT_TM // OUT_RC):
        rows = slice(r * OUT_RC, (r + 1) * OUT_RC)
        o_ref[rows, :] = jnp.dot(a_ref[rows, :], w_ref[...], preferred_element_type=F32).astype(BF16)


def _outproj(a, w):
    m, kdim = a.shape
    n = w.shape[1]
    return pl.pallas_call(
        _outproj_kernel,
        grid=(m // OUT_TM, n // OUT_TN),
        in_specs=[pl.BlockSpec((OUT_TM, kdim), lambda i, j: (i, 0)),
                  pl.BlockSpec((kdim, OUT_TN), lambda i, j: (0, j))],
        out_specs=pl.BlockSpec((OUT_TM, OUT_TN), lambda i, j: (i, j)),
        out_shape=jax.ShapeDtypeStruct((m, n), BF16),
        compiler_params=_params(("arbitrary", "arbitrary"), 48 * MIB),
        name="outproj",
    )(a, w)


LN_TR = 256


def _residual_ln(x, branch, gate, ln_g, ln_b):
    y = ALPHA * x + (1.0 + gate) * branch.astype(F32)
    mu = jnp.mean(y, axis=-1, keepdims=True)
    yc = y - mu
    var = jnp.mean(yc * yc, axis=-1, keepdims=True)
    return yc * lax.rsqrt(var + LN_EPS) * ln_g + ln_b


def _ln1_kernel(x_ref, br_ref, gate_ref, lng_ref, lnb_ref, sc_ref, sh_ref, x1_ref, h2_ref):
    x1 = _residual_ln(x_ref[...], br_ref[...], gate_ref[0], lng_ref[...], lnb_ref[...])
    x1_ref[...] = x1
    h2_ref[...] = (x1 * (1.0 + sc_ref[0]) + sh_ref[0]).astype(BF16)


def _ln2_kernel(x_ref, br_ref, gate_ref, lng_ref, lnb_ref, o_ref):
    o_ref[...] = _residual_ln(x_ref[...], br_ref[...], gate_ref[0], lng_ref[...], lnb_ref[...])


def _ln_call(kernel, name, x, branch, batch_vecs, row_vecs, out_dtypes, seq):
    m, d = x.shape
    per_batch = seq // LN_TR
    tile = pl.BlockSpec((LN_TR, d), lambda i: (i, 0))
    bvec = pl.BlockSpec((1, 1, d), lambda i: (i // per_batch, 0, 0))
    rvec = pl.BlockSpec((1, d), lambda i: (0, 0))
    operands = [x, branch, batch_vecs[0], *row_vecs, *batch_vecs[1:]]
    specs = [tile, tile, bvec] + [rvec] * len(row_vecs) + [bvec] * (len(batch_vecs) - 1)
    return pl.pallas_call(
        kernel,
        grid=(m // LN_TR,),
        in_specs=specs,
        out_specs=[tile] * len(out_dtypes),
        out_shape=[jax.ShapeDtypeStruct((m, d), dt) for dt in out_dtypes],
        compiler_params=_params(("arbitrary",), 48 * MIB),
        name=name,
    )(*operands)


UP_TM = 4096
UP_TF = 256
UP_RC = 512
DOWN_TM = 512
DOWN_TN = 1024
DOWN_RC = 256
RESIDENT = pl.Buffered(1)


def _ffn_up_kernel(h_ref, wg_ref, wu_ref, cast_src, o_ref, cast_dst):
    _HostedCast.run(cast_src, cast_dst)
    for r in range(UP_TM // UP_RC):
        rows = slice(r * UP_RC, (r + 1) * UP_RC)
        h = h_ref[rows, :]
        a_gate = jnp.dot(h, wg_ref[...], preferred_element_type=F32)
        a_up = jnp.dot(h, wu_ref[...], preferred_element_type=F32)
        o_ref[rows, :] = (jax.nn.silu(a_gate) * a_up).astype(BF16)


def _ffn_up(h2, w_gate_up, hosted_weight, hosted_rows):
    m, d = h2.shape
    d_ff = w_gate_up.shape[1] // 2
    nf = d_ff // UP_TF
    hosted = _HostedCast(hosted_weight, hosted_rows, 0, lambda i, f: i * nf + f)
    assert hosted.num_blocks <= (m // UP_TM) * nf
    return pl.pallas_call(
        _ffn_up_kernel,
        grid=(m // UP_TM, nf),
        in_specs=[pl.BlockSpec((UP_TM, d), lambda i, f: (i, 0), pipeline_mode=RESIDENT),
                  pl.BlockSpec((d, UP_TF), lambda i, f: (0, f)),
                  pl.BlockSpec((d, UP_TF), lambda i, f: (0, nf + f)),
                  hosted.spec],
        out_specs=[pl.BlockSpec((UP_TM, UP_TF), lambda i, f: (i, f)), hosted.spec],
        out_shape=[jax.ShapeDtypeStruct((m, d_ff), BF16), hosted.out_shape],
        compiler_params=_params(("arbitrary", "arbitrary"), V7X_VMEM_BUDGET),
        name="ffn_up",
    )(h2, w_gate_up, w_gate_up, hosted_weight)


def _ffn_down_kernel(a_ref, w_ref, o_ref):
    for r in range(DOWN_TM // DOWN_RC):
        rows = slice(r * DOWN_RC, (r + 1) * DOWN_RC)
        o_ref[rows, :] = jnp.dot(a_ref[rows, :], w_ref[...], preferred_element_type=F32).astype(BF16)


def _ffn_down(act, w_down):
    m, d_ff = act.shape
    d = w_down.shape[1]
    return pl.pallas_call(
        _ffn_down_kernel,
        grid=(d // DOWN_TN, m // DOWN_TM),
        in_specs=[pl.BlockSpec((DOWN_TM, d_ff), lambda j, i: (i, 0)),
                  pl.BlockSpec((d_ff, DOWN_TN), lambda j, i: (0, j), pipeline_mode=RESIDENT)],
        out_specs=pl.BlockSpec((DOWN_TM, DOWN_TN), lambda j, i: (i, j)),
        out_shape=jax.ShapeDtypeStruct((m, d), BF16),
        compiler_params=_params(("arbitrary", "arbitrary"), V7X_VMEM_BUDGET),
        name="ffn_down",
    )(act, w_down)


def _decay_tables():
    log_g = jnp.log(1.0 - 2.0 ** (-5.0 - jnp.arange(RET_HEADS, dtype=F32)))
    idx = jnp.arange(CHUNK, dtype=F32)
    diff = idx[:, None] - idx[None, :]
    inner = jnp.where((diff >= 0)[None],
                      jnp.exp(log_g[:, None, None] * jnp.maximum(diff, 0.0)[None]), 0.0)
    xi = jnp.exp(log_g[:, None] * (idx[None, :] + 1.0))
    zeta = jnp.exp(log_g[:, None] * (CHUNK - 1.0 - idx[None, :]))
    chunk_decay = jnp.exp(log_g * CHUNK)
    xi_b = jnp.broadcast_to(xi[:, :, None], (RET_HEADS, CHUNK, HEAD_DIM))
    zeta_b = jnp.broadcast_to(zeta[:, :, None], (RET_HEADS, CHUNK, HEAD_DIM))
    return chunk_decay, inner, xi_b, zeta_b


def kernel(x, c, positions, w_ada, b_ada, w_in, sg_ln_g, sg_ln_b, sg_w, sg_b, w_ret_up, w_sg_up,
           w_out, ln1_g, ln1_b, w_gate_up, w_down, ln2_g, ln2_b):
    batch, seq, d = x.shape
    m = batch * seq
    assert w_ada.shape[0] == DEPTH == 1
    x2 = x.reshape(m, d)

    c_pad = jnp.zeros((8, d), F32).at[:batch].set(c)
    ada = _ada(c_pad, w_ada[0], b_ada[0][None, :])[:batch]
    sh1, sc1, g1, sh2, sc2, g2 = [t[:, None, :] for t in jnp.split(ada, 6, axis=-1)]

    h1 = _modulate(x2, sc1, sh1, seq)

    inv_freq = (ROPE_BASE ** (-jnp.arange(0, HEAD_DIM, 2, dtype=F32) / HEAD_DIM))[None, :]
    cos, sin = _rope_tables(positions.astype(F32).reshape(m, 1), inv_freq)

    z, (w_gate_up_bf, w_out_bf, w_ret_up_bf, w_sg_up_bf) = _zproj(
        h1, w_in[0], cos, sin,
        [(w_gate_up[0], 32, 0), (w_out[0], 32, 0), (w_ret_up[0], 16, 40), (w_sg_up[0], 16, 40)])

    chunk_decay, inner_decay, xi_b, zeta_b = _decay_tables()
    ret, sgu = _mixers(z, chunk_decay, inner_decay, xi_b, zeta_b, sg_ln_g[0][None, :],
                       sg_ln_b[0][None, :], sg_w[0], sg_b[0][:, :, None], batch, seq)

    gate_col0 = 4 * RET_HEADS * HEAD_DIM + 2 * SG_GROUPS * SG_GROUP_DIM
    mix = _mix(ret, sgu, w_ret_up_bf, w_sg_up_bf, z, gate_col0)
    branch1 = _outproj(mix, w_out_bf)
    x1, h2 = _ln_call(_ln1_kernel, "ln1", x2, branch1, [g1, sc2, sh2],
                      [ln1_g[0][None, :], ln1_b[0][None, :]], [F32, BF16], seq)

    act, w_down_bf = _ffn_up(h2, w_gate_up_bf, w_down[0], 128)
    ffn = _ffn_down(act, w_down_bf)
    (out,) = _ln_call(_ln2_kernel, "ln2", x1, ffn, [g2],
                      [ln2_g[0][None, :], ln2_b[0][None, :]], [F32], seq)
    return out.reshape(batch, seq, d)
```

```python
import functools

import jax
import jax.numpy as jnp
from jax import lax
from jax.experimental import pallas as pl
from jax.experimental.pallas import tpu as pltpu

F32 = jnp.float32
BF16 = jnp.bfloat16

RET_HEADS = 8
HEAD_DIM = 256
SG_GROUPS = 8
SG_GROUP_DIM = 256
CHUNK = 128
ROPE_BASE = 10000.0
LN_EPS = 1e-5
DEPTH = 1
ALPHA = (2.0 * DEPTH) ** 0.25

MIB = 1024 * 1024
V7X_VMEM_BUDGET = 56 * MIB


def _params(sem, vmem_bytes):
    return pltpu.CompilerParams(dimension_semantics=sem, vmem_limit_bytes=vmem_bytes)


class _HostedCast:
    def __init__(self, weight, rows_per_block, first_step, step_of):
        rows, cols = weight.shape
        self.weight = weight
        self.first_step = first_step
        self.num_blocks = rows // rows_per_block
        assert self.num_blocks * rows_per_block == rows
        last = self.num_blocks - 1

        def index_map(*grid_ids):
            return (jnp.clip(step_of(*grid_ids) - first_step, 0, last), 0)

        self.spec = pl.BlockSpec((rows_per_block, cols), index_map)
        self.out_shape = jax.ShapeDtypeStruct((rows, cols), BF16)

    @staticmethod
    def run(src_ref, dst_ref):
        dst_ref[...] = src_ref[...].astype(BF16)


ADA_TN = 512


def _ada_columns(c_ref, w_ref, b_ref):
    s = jax.nn.silu(c_ref[...]).astype(BF16)
    return jnp.dot(s, w_ref[...].astype(BF16), preferred_element_type=F32) + b_ref[...]


def _ada_kernel(c_ref, w_ref, b_ref, o_ref):
    o_ref[...] = _ada_columns(c_ref, w_ref, b_ref)


def _ada(c_pad, w_ada, b_ada, n):
    rows, d = c_pad.shape
    return pl.pallas_call(
        _ada_kernel,
        grid=(n // ADA_TN,),
        in_specs=[pl.BlockSpec((rows, d), lambda j: (0, 0)),
                  pl.BlockSpec((d, ADA_TN), lambda j: (0, j)),
                  pl.BlockSpec((1, ADA_TN), lambda j: (0, j))],
        out_specs=pl.BlockSpec((rows, ADA_TN), lambda j: (0, j)),
        out_shape=jax.ShapeDtypeStruct((rows, n), F32),
        compiler_params=_params(("arbitrary",), 32 * MIB),
        name="ada",
    )(c_pad, w_ada, b_ada)


MOD_TR = 512


def _modulate_kernel(x_ref, sc_ref, sh_ref, o_ref):
    o_ref[...] = (x_ref[...] * (1.0 + sc_ref[0]) + sh_ref[0]).astype(BF16)


def _modulate(x2, sc, sh, seq):
    m, d = x2.shape
    per_batch = seq // MOD_TR
    vec = pl.BlockSpec((1, 1, d), lambda i: (i // per_batch, 0, 0))
    return pl.pallas_call(
        _modulate_kernel,
        grid=(m // MOD_TR,),
        in_specs=[pl.BlockSpec((MOD_TR, d), lambda i: (i, 0)), vec, vec],
        out_specs=pl.BlockSpec((MOD_TR, d), lambda i: (i, 0)),
        out_shape=jax.ShapeDtypeStruct((m, d), BF16),
        compiler_params=_params(("arbitrary",), 40 * MIB),
        name="modulate",
    )(x2, sc, sh)


ROPE_TR = 1024


def _rope_kernel(pos_ref, inv_ref, cos_ref, sin_ref):
    ang = pos_ref[...] * inv_ref[...]
    cos_ref[...] = jnp.cos(ang)
    sin_ref[...] = jnp.sin(ang)


def _rope_tables(pos_col, inv_freq):
    m = pos_col.shape[0]
    half = inv_freq.shape[1]
    out = jax.ShapeDtypeStruct((m, half), F32)
    return pl.pallas_call(
        _rope_kernel,
        grid=(m // ROPE_TR,),
        in_specs=[pl.BlockSpec((ROPE_TR, 1), lambda i: (i, 0)),
                  pl.BlockSpec((1, half), lambda i: (0, 0))],
        out_specs=[pl.BlockSpec((ROPE_TR, half), lambda i: (i, 0))] * 2,
        out_shape=[out, out],
        compiler_params=_params(("arbitrary",), 32 * MIB),
        name="rope",
    )(pos_col, inv_freq)


Z_TM = 1024
Z_TN = 1024
Z_RC = 256


def _zproj_kernel(*refs, nt, mt, bounds, hosted):
    nh = len(hosted)
    h_ref, w_ref, cos_ref, sin_ref = refs[:4]
    hosted_src = refs[4:4 + nh]
    o_ref = refs[4 + nh]
    hosted_dst = refs[5 + nh:5 + 2 * nh]
    wcur_ref, wnext_ref = refs[5 + 2 * nh:7 + 2 * nh]
    n = pl.program_id(0)
    m = pl.program_id(1)
    half = HEAD_DIM // 2
    kc = w_ref.shape[0]

    def stage_and_host():
        wnext_ref[pl.ds(pl.multiple_of(m * kc, kc), kc), :] = w_ref[...].astype(BF16)
        for cast, src, dst in zip(hosted, hosted_src, hosted_dst):
            cast.run(src, dst)

    @pl.when(n == 0)
    def _no_tile_yet():
        stage_and_host()

    t = n - 1

    def run(epilogue):
        stage_and_host()
        for r in range(Z_TM // Z_RC):
            rows = slice(r * Z_RC, (r + 1) * Z_RC)
            acc = jnp.dot(h_ref[rows, :], wcur_ref[...], preferred_element_type=F32)
            epilogue(acc, rows)

    def rotary(acc, rows):
        scale = jnp.where(t < q_end, 1.0, HEAD_DIM ** -0.5).astype(F32)
        c = cos_ref[rows, :]
        s = sin_ref[rows, :]
        for j in range(Z_TN // HEAD_DIM):
            x1 = acc[:, j * HEAD_DIM: j * HEAD_DIM + half]
            x2 = acc[:, j * HEAD_DIM + half: (j + 1) * HEAD_DIM]
            o_ref[rows, j * HEAD_DIM: j * HEAD_DIM + half] = ((x1 * c - x2 * s) * scale).astype(BF16)
            o_ref[rows, j * HEAD_DIM + half: (j + 1) * HEAD_DIM] = ((x2 * c + x1 * s) * scale).astype(BF16)

    def pointwise(fn):
        def epi(acc, rows):
            o_ref[rows, :] = fn(acc).astype(BF16)
        return epi

    q_end, k_end, v_end, g_end, us_end = bounds
    pl.when((n >= 1) & (t < k_end))(lambda: run(rotary))
    pl.when((t >= k_end) & (t < g_end))(
        lambda: run(pointwise(lambda a: jnp.where(t < v_end, a, jax.nn.silu(a)))))
    pl.when((t >= g_end) & (t < us_end))(lambda: run(pointwise(jax.nn.gelu)))
    pl.when(t >= us_end)(lambda: run(pointwise(jax.nn.sigmoid)))

    @pl.when(m == mt - 1)
    def _sweep_done():
        wcur_ref[...] = wnext_ref[...]


def _zproj(h1, w_in, cos, sin, hosted_weights):
    m, d = h1.shape
    n = w_in.shape[1]
    nt, mt = n // Z_TN, m // Z_TM
    kc = d // mt
    ret_w = RET_HEADS * HEAD_DIM
    sg_w = SG_GROUPS * SG_GROUP_DIM
    bounds = tuple(b // Z_TN for b in (ret_w, 2 * ret_w, 3 * ret_w, 4 * ret_w, 4 * ret_w + 2 * sg_w))
    half = cos.shape[1]
    hosted = [_HostedCast(w, rb, first, lambda j, i: j * mt + i) for w, rb, first in hosted_weights]
    assert all(c.first_step + c.num_blocks <= (nt + 1) * mt for c in hosted)

    def row_tile(j, i):
        return (jnp.where(j == 0, 0, i), 0)

    def out_tile(j, i):
        return (jnp.where(j == 0, 0, i), jnp.maximum(j - 1, 0))

    outs = pl.pallas_call(
        functools.partial(_zproj_kernel, nt=nt, mt=mt, bounds=bounds, hosted=hosted),
        grid=(nt + 1, mt),
        in_specs=[pl.BlockSpec((Z_TM, d), row_tile),
                  pl.BlockSpec((kc, Z_TN), lambda j, i: (i, jnp.minimum(j, nt - 1))),
                  pl.BlockSpec((Z_TM, half), row_tile),
                  pl.BlockSpec((Z_TM, half), row_tile)] + [c.spec for c in hosted],
        out_specs=[pl.BlockSpec((Z_TM, Z_TN), out_tile)]
                  + [c.spec for c in hosted],
        out_shape=[jax.ShapeDtypeStruct((m, n), BF16)] + [c.out_shape for c in hosted],
        scratch_shapes=[pltpu.VMEM((d, Z_TN), BF16), pltpu.VMEM((d, Z_TN), BF16)],
        compiler_params=_params(("arbitrary", "arbitrary"), V7X_VMEM_BUDGET),
        name="zproj",
    )(h1, w_in, cos, sin, *[c.weight for c in hosted])
    return outs[0], outs[1:]


def _retention_chunk(cd_ref, q_ref, k_ref, v_ref, g_ref, dec_ref, xi_ref, zeta_ref, o_ref, state_ref):
    for h in range(RET_HEADS):
        cols = slice(h * HEAD_DIM, (h + 1) * HEAD_DIM)
        q = q_ref[:, cols]
        k = k_ref[:, cols]
        v = v_ref[:, cols]
        scores = lax.dot_general(q, k, (((1,), (1,)), ((), ())), preferred_element_type=F32)
        scores = scores * dec_ref[h]
        inner = jnp.dot(scores.astype(BF16), v, preferred_element_type=F32)
        state = state_ref[h]
        cross = jnp.dot(q, state.astype(BF16), preferred_element_type=F32) * xi_ref[h]
        kz = (k.astype(F32) * zeta_ref[h]).astype(BF16)
        update = lax.dot_general(kz, v, (((0,), (0,)), ((), ())), preferred_element_type=F32)
        state_ref[h] = cd_ref[h] * state + update
        y = inner + cross
        mu = jnp.mean(y, axis=-1, keepdims=True)
        yc = y - mu
        var = jnp.mean(yc * yc, axis=-1, keepdims=True)
        yn = yc * lax.rsqrt(var + LN_EPS)
        o_ref[:, cols] = (yn * g_ref[:, cols].astype(F32)).astype(BF16)


def _sgu_kernel(u_ref, s_ref, lng_ref, lnb_ref, w_ref, b_ref, o_ref):
    s = s_ref[...].astype(F32)
    mu = jnp.mean(s, axis=-1, keepdims=True)
    sc = s - mu
    var = jnp.mean(sc * sc, axis=-1, keepdims=True)
    sn = (sc * lax.rsqrt(var + LN_EPS) * lng_ref[...] + lnb_ref[...]).astype(BF16)
    t_idx = lax.broadcasted_iota(jnp.int32, (CHUNK, CHUNK), 0)
    s_idx = lax.broadcasted_iota(jnp.int32, (CHUNK, CHUNK), 1)
    causal = t_idx >= s_idx
    for g in range(SG_GROUPS):
        cols = slice(g * SG_GROUP_DIM, (g + 1) * SG_GROUP_DIM)
        w = jnp.where(causal, w_ref[g], 0.0).astype(BF16)
        mixed = jnp.dot(w, sn[:, cols], preferred_element_type=F32) + b_ref[g]
        o_ref[:, cols] = (u_ref[:, cols].astype(F32) * mixed).astype(BF16)


def _mixers_kernel(cd_ref, q_ref, k_ref, v_ref, g_ref, dec_ref, xi_ref, zeta_ref,
                   u_ref, s_ref, lng_ref, lnb_ref, w_ref, b_ref, ret_ref, sgu_ref, state_ref):
    @pl.when(pl.program_id(0) == 0)
    def _():
        state_ref[...] = jnp.zeros_like(state_ref)

    for b in range(q_ref.shape[0]):
        _retention_chunk(cd_ref, q_ref.at[b], k_ref.at[b], v_ref.at[b], g_ref.at[b], dec_ref, xi_ref,
                         zeta_ref, ret_ref.at[b], state_ref.at[b])
        _sgu_kernel(u_ref.at[b], s_ref.at[b], lng_ref, lnb_ref, w_ref, b_ref, sgu_ref.at[b])


def _mixers(z, chunk_decay, inner_decay, xi_b, zeta_b, ln_g, ln_b, w_s, b_col, batch, seq):
    m, z_cols = z.shape
    width = RET_HEADS * HEAD_DIM
    assert width == SG_GROUPS * SG_GROUP_DIM
    z3 = z.reshape(batch, seq, z_cols)

    def zblock(col):
        return pl.BlockSpec((batch, CHUNK, width), lambda c: (0, c, col))

    def const(shape):
        return pl.BlockSpec(shape, lambda c: (0,) * len(shape))

    out_spec = pl.BlockSpec((batch, CHUNK, width), lambda c: (0, c, 0))
    out_shape = jax.ShapeDtypeStruct((batch, seq, width), BF16)
    ret, sgu = pl.pallas_call(
        _mixers_kernel,
        grid=(seq // CHUNK,),
        in_specs=[pl.BlockSpec(memory_space=pltpu.SMEM),
                  zblock(0), zblock(1), zblock(2), zblock(3),
                  const(inner_decay.shape), const(xi_b.shape), const(zeta_b.shape),
                  zblock(4), zblock(5), const(ln_g.shape), const(ln_b.shape),
                  const(w_s.shape), const(b_col.shape)],
        out_specs=[out_spec, out_spec],
        out_shape=[out_shape, out_shape],
        scratch_shapes=[pltpu.VMEM((batch, RET_HEADS, HEAD_DIM, HEAD_DIM), F32)],
        compiler_params=_params(("arbitrary",), 32 * MIB),
        name="mixers",
    )(chunk_decay, z3, z3, z3, z3, inner_decay, xi_b, zeta_b, z3, z3, ln_g, ln_b, w_s, b_col)
    return ret.reshape(m, width), sgu.reshape(m, width)


MIX_TM = 1024
MIX_TN = 1024
MIX_RC = 256


def _mix_kernel(ret_ref, sgu_ref, wr_ref, ws_ref, gr_ref, gs_ref, o_ref):
    for r in range(MIX_TM // MIX_RC):
        rows = slice(r * MIX_RC, (r + 1) * MIX_RC)
        y_ret = jnp.dot(ret_ref[rows, :], wr_ref[...], preferred_element_type=F32)
        y_sg = jnp.dot(sgu_ref[rows, :], ws_ref[...], preferred_element_type=F32)
        mix = gr_ref[rows, :].astype(F32) * y_ret + gs_ref[rows, :].astype(F32) * y_sg
        o_ref[rows, :] = mix.astype(BF16)


def _mix(ret, sgu, w_ret_up, w_sg_up, z, gate_col0):
    m, kdim = ret.shape
    n = w_ret_up.shape[1]
    nt = n // MIX_TN
    g0 = gate_col0 // MIX_TN
    g1 = g0 + nt
    return pl.pallas_call(
        _mix_kernel,
        grid=(m // MIX_TM, nt),
        in_specs=[pl.BlockSpec((MIX_TM, kdim), lambda i, j: (i, 0)),
                  pl.BlockSpec((MIX_TM, kdim), lambda i, j: (i, 0)),
                  pl.BlockSpec((kdim, MIX_TN), lambda i, j: (0, j)),
                  pl.BlockSpec((kdim, MIX_TN), lambda i, j: (0, j)),
                  pl.BlockSpec((MIX_TM, MIX_TN), lambda i, j: (i, g0 + j)),
                  pl.BlockSpec((MIX_TM, MIX_TN), lambda i, j: (i, g1 + j))],
        out_specs=pl.BlockSpec((MIX_TM, MIX_TN), lambda i, j: (i, j)),
        out_shape=jax.ShapeDtypeStruct((m, n), BF16),
        compiler_params=_params(("arbitrary", "arbitrary"), V7X_VMEM_BUDGET),
        name="mix",
    )(ret, sgu, w_ret_up, w_sg_up, z, z)


OUT_TM = 1024
OUT_TN = 1024
OUT_RC = 256


def _outproj_kernel(a_ref, w_ref, c_ref, wa_ref, ba_ref, o_ref, ada_ref):
    ada_ref[...] = _ada_columns(c_ref, wa_ref, ba_ref)
    for r in range(OUT_TM // OUT_RC):
        rows = slice(r * OUT_RC, (r + 1) * OUT_RC)
        o_ref[rows, :] = jnp.dot(a_ref[rows, :], w_ref[...], preferred_element_type=F32).astype(BF16)


def _outproj(a, w, c_pad, w_ada, b_ada, ada_col0):
    m, kdim = a.shape
    n = w.shape[1]
    nt = n // OUT_TN
    steps = (m // OUT_TM) * nt
    assert ada_col0 % ADA_TN == 0 and ada_col0 + steps * ADA_TN == w_ada.shape[1]
    block0 = ada_col0 // ADA_TN
    rows, d = c_pad.shape
    return pl.pallas_call(
        _outproj_kernel,
        grid=(m // OUT_TM, nt),
        in_specs=[pl.BlockSpec((OUT_TM, kdim), lambda i, j: (i, 0)),
                  pl.BlockSpec((kdim, OUT_TN), lambda i, j: (0, j)),
                  pl.BlockSpec((rows, d), lambda i, j: (0, 0)),
                  pl.BlockSpec((d, ADA_TN), lambda i, j: (0, block0 + i * nt + j)),
                  pl.BlockSpec((1, ADA_TN), lambda i, j: (0, block0 + i * nt + j))],
        out_specs=[pl.BlockSpec((OUT_TM, OUT_TN), lambda i, j: (i, j)),
                   pl.BlockSpec((rows, ADA_TN), lambda i, j: (0, i * nt + j))],
        out_shape=[jax.ShapeDtypeStruct((m, n), BF16),
                   jax.ShapeDtypeStruct((rows, steps * ADA_TN), F32)],
        compiler_params=_params(("arbitrary", "arbitrary"), V7X_VMEM_BUDGET),
        name="outproj",
    )(a, w, c_pad, w_ada, b_ada)


LN_TR = 256


def _residual_ln(x, branch, gate, ln_g, ln_b):
    y = ALPHA * x + (1.0 + gate) * branch.astype(F32)
    mu = jnp.mean(y, axis=-1, keepdims=True)
    yc = y - mu
    var = jnp.mean(yc * yc, axis=-1, keepdims=True)
    return yc * lax.rsqrt(var + LN_EPS) * ln_g + ln_b


def _ln1_kernel(x_ref, br_ref, gate_ref, lng_ref, lnb_ref, sc_ref, sh_ref, x1_ref, h2_ref):
    x1 = _residual_ln(x_ref[...], br_ref[...], gate_ref[0], lng_ref[...], lnb_ref[...])
    x1_ref[...] = x1
    h2_ref[...] = (x1 * (1.0 + sc_ref[0]) + sh_ref[0]).astype(BF16)


def _ln2_kernel(x_ref, br_ref, gate_ref, lng_ref, lnb_ref, o_ref):
    o_ref[...] = _residual_ln(x_ref[...], br_ref[...], gate_ref[0], lng_ref[...], lnb_ref[...])


def _ln_call(kernel, name, x, branch, batch_vecs, row_vecs, out_dtypes, seq):
    m, d = x.shape
    per_batch = seq // LN_TR
    tile = pl.BlockSpec((LN_TR, d), lambda i: (i, 0))
    bvec = pl.BlockSpec((1, 1, d), lambda i: (i // per_batch, 0, 0))
    rvec = pl.BlockSpec((1, d), lambda i: (0, 0))
    operands = [x, branch, batch_vecs[0], *row_vecs, *batch_vecs[1:]]
    specs = [tile, tile, bvec] + [rvec] * len(row_vecs) + [bvec] * (len(batch_vecs) - 1)
    return pl.pallas_call(
        kernel,
        grid=(m // LN_TR,),
        in_specs=specs,
        out_specs=[tile] * len(out_dtypes),
        out_shape=[jax.ShapeDtypeStruct((m, d), dt) for dt in out_dtypes],
        compiler_params=_params(("arbitrary",), 48 * MIB),
        name=name,
    )(*operands)


UP_TM = 4096
UP_TF = 256
UP_RC = 512
DOWN_TM = 512
DOWN_TN = 1024
DOWN_RC = 256
RESIDENT = pl.Buffered(1)


def _ffn_up_kernel(h_ref, wg_ref, wu_ref, cast_src, o_ref, cast_dst):
    _HostedCast.run(cast_src, cast_dst)
    for r in range(UP_TM // UP_RC):
        rows = slice(r * UP_RC, (r + 1) * UP_RC)
        h = h_ref[rows, :]
        a_gate = jnp.dot(h, wg_ref[...], preferred_element_type=F32)
        a_up = jnp.dot(h, wu_ref[...], preferred_element_type=F32)
        o_ref[rows, :] = (jax.nn.silu(a_gate) * a_up).astype(BF16)


def _ffn_up(h2, w_gate_up, hosted_weight, hosted_rows):
    m, d = h2.shape
    d_ff = w_gate_up.shape[1] // 2
    nf = d_ff // UP_TF
    hosted = _HostedCast(hosted_weight, hosted_rows, 0, lambda i, f: i * nf + f)
    assert hosted.num_blocks <= (m // UP_TM) * nf
    return pl.pallas_call(
        _ffn_up_kernel,
        grid=(m // UP_TM, nf),
        in_specs=[pl.BlockSpec((UP_TM, d), lambda i, f: (i, 0), pipeline_mode=RESIDENT),
                  pl.BlockSpec((d, UP_TF), lambda i, f: (0, f)),
                  pl.BlockSpec((d, UP_TF), lambda i, f: (0, nf + f)),
                  hosted.spec],
        out_specs=[pl.BlockSpec((UP_TM, UP_TF), lambda i, f: (i, f)), hosted.spec],
        out_shape=[jax.ShapeDtypeStruct((m, d_ff), BF16), hosted.out_shape],
        compiler_params=_params(("arbitrary", "arbitrary"), V7X_VMEM_BUDGET),
        name="ffn_up",
    )(h2, w_gate_up, w_gate_up, hosted_weight)


def _ffn_down_kernel(a_ref, w_ref, o_ref):
    for r in range(DOWN_TM // DOWN_RC):
        rows = slice(r * DOWN_RC, (r + 1) * DOWN_RC)
        o_ref[rows, :] = jnp.dot(a_ref[rows, :], w_ref[...], preferred_element_type=F32).astype(BF16)


def _ffn_down(act, w_down):
    m, d_ff = act.shape
    d = w_down.shape[1]
    return pl.pallas_call(
        _ffn_down_kernel,
        grid=(d // DOWN_TN, m // DOWN_TM),
        in_specs=[pl.BlockSpec((DOWN_TM, d_ff), lambda j, i: (i, 0)),
                  pl.BlockSpec((d_ff, DOWN_TN), lambda j, i: (0, j), pipeline_mode=RESIDENT)],
        out_specs=pl.BlockSpec((DOWN_TM, DOWN_TN), lambda j, i: (i, j)),
        out_shape=jax.ShapeDtypeStruct((m, d), BF16),
        compiler_params=_params(("arbitrary", "arbitrary"), V7X_VMEM_BUDGET),
        name="ffn_down",
    )(act, w_down)


def _decay_tables():
    log_g = jnp.log(1.0 - 2.0 ** (-5.0 - jnp.arange(RET_HEADS, dtype=F32)))
    idx = jnp.arange(CHUNK, dtype=F32)
    diff = idx[:, None] - idx[None, :]
    inner = jnp.where((diff >= 0)[None],
                      jnp.exp(log_g[:, None, None] * jnp.maximum(diff, 0.0)[None]), 0.0)
    xi = jnp.exp(log_g[:, None] * (idx[None, :] + 1.0))
    zeta = jnp.exp(log_g[:, None] * (CHUNK - 1.0 - idx[None, :]))
    chunk_decay = jnp.exp(log_g * CHUNK)
    xi_b = jnp.broadcast_to(xi[:, :, None], (RET_HEADS, CHUNK, HEAD_DIM))
    zeta_b = jnp.broadcast_to(zeta[:, :, None], (RET_HEADS, CHUNK, HEAD_DIM))
    return chunk_decay, inner, xi_b, zeta_b


def kernel(x, c, positions, w_ada, b_ada, w_in, sg_ln_g, sg_ln_b, sg_w, sg_b, w_ret_up, w_sg_up,
           w_out, ln1_g, ln1_b, w_gate_up, w_down, ln2_g, ln2_b):
    batch, seq, d = x.shape
    m = batch * seq
    assert w_ada.shape[0] == DEPTH == 1
    x2 = x.reshape(m, d)

    c_pad = jnp.zeros((8, d), F32).at[:batch].set(c)
    b_ada_row = b_ada[0][None, :]
    sh1, sc1 = [t[:, None, :] for t in jnp.split(_ada(c_pad, w_ada[0], b_ada_row, 2 * d)[:batch], 2, axis=-1)]

    h1 = _modulate(x2, sc1, sh1, seq)

    inv_freq = (ROPE_BASE ** (-jnp.arange(0, HEAD_DIM, 2, dtype=F32) / HEAD_DIM))[None, :]
    cos, sin = _rope_tables(positions.astype(F32).reshape(m, 1), inv_freq)

    z, (w_gate_up_bf, w_out_bf, w_ret_up_bf, w_sg_up_bf) = _zproj(
        h1, w_in[0], cos, sin,
        [(w_gate_up[0], 32, 0), (w_out[0], 32, 0), (w_ret_up[0], 16, 40), (w_sg_up[0], 16, 40)])

    chunk_decay, inner_decay, xi_b, zeta_b = _decay_tables()
    ret, sgu = _mixers(z, chunk_decay, inner_decay, xi_b, zeta_b, sg_ln_g[0][None, :],
                       sg_ln_b[0][None, :], sg_w[0], sg_b[0][:, :, None], batch, seq)

    gate_col0 = 4 * RET_HEADS * HEAD_DIM + 2 * SG_GROUPS * SG_GROUP_DIM
    mix = _mix(ret, sgu, w_ret_up_bf, w_sg_up_bf, z, gate_col0)
    branch1, ada_late = _outproj(mix, w_out_bf, c_pad, w_ada[0], b_ada_row, 2 * d)
    g1, sh2, sc2, g2 = [t[:, None, :] for t in jnp.split(ada_late[:batch], 4, axis=-1)]
    x1, h2 = _ln_call(_ln1_kernel, "ln1", x2, branch1, [g1, sc2, sh2],
                      [ln1_g[0][None, :], ln1_b[0][None, :]], [F32, BF16], seq)

    act, w_down_bf = _ffn_up(h2, w_gate_up_bf, w_down[0], 128)
    ffn = _ffn_down(act, w_down_bf)
    (out,) = _ln_call(_ln2_kernel, "ln2", x1, ffn, [g2],
                      [ln2_g[0][None, :], ln2_b[0][None, :]], [F32], seq)
    return out.reshape(batch, seq, d)
```

```python
import functools

import jax
import jax.numpy as jnp
from jax import lax
from jax.experimental import pallas as pl
from jax.experimental.pallas import tpu as pltpu

F32 = jnp.float32
BF16 = jnp.bfloat16

RET_HEADS = 8
HEAD_DIM = 256
SG_GROUPS = 8
SG_GROUP_DIM = 256
CHUNK = 128
ROPE_BASE = 10000.0
LN_EPS = 1e-5
DEPTH = 1
ALPHA = (2.0 * DEPTH) ** 0.25

MIB = 1024 * 1024
V7X_VMEM_BUDGET = 56 * MIB


def _params(sem, vmem_bytes):
    return pltpu.CompilerParams(dimension_semantics=sem, vmem_limit_bytes=vmem_bytes)


class _HostedCast:
    def __init__(self, weight, rows_per_block, first_step, step_of):
        rows, cols = weight.shape
        self.weight = weight
        self.first_step = first_step
        self.num_blocks = rows // rows_per_block
        assert self.num_blocks * rows_per_block == rows
        last = self.num_blocks - 1

        def index_map(*grid_ids):
            return (jnp.clip(step_of(*grid_ids) - first_step, 0, last), 0)

        self.spec = pl.BlockSpec((rows_per_block, cols), index_map)
        self.out_shape = jax.ShapeDtypeStruct((rows, cols), BF16)

    @staticmethod
    def run(src_ref, dst_ref):
        dst_ref[...] = src_ref[...].astype(BF16)


ADA_TN = 512


def _ada_columns(c_ref, w_ref, b_ref):
    s = jax.nn.silu(c_ref[...]).astype(BF16)
    return jnp.dot(s, w_ref[...].astype(BF16), preferred_element_type=F32) + b_ref[...]


def _ada_kernel(c_ref, w_ref, b_ref, o_ref):
    o_ref[...] = _ada_columns(c_ref, w_ref, b_ref)


def _ada(c_pad, w_ada, b_ada, n):
    rows, d = c_pad.shape
    return pl.pallas_call(
        _ada_kernel,
        grid=(n // ADA_TN,),
        in_specs=[pl.BlockSpec((rows, d), lambda j: (0, 0)),
                  pl.BlockSpec((d, ADA_TN), lambda j: (0, j)),
                  pl.BlockSpec((1, ADA_TN), lambda j: (0, j))],
        out_specs=pl.BlockSpec((rows, ADA_TN), lambda j: (0, j)),
        out_shape=jax.ShapeDtypeStruct((rows, n), F32),
        compiler_params=_params(("arbitrary",), 32 * MIB),
        name="ada",
    )(c_pad, w_ada, b_ada)


MOD_TR = 1024


def _modulate_kernel(x_ref, sc_ref, sh_ref, o_ref):
    o_ref[...] = (x_ref[...] * (1.0 + sc_ref[0]) + sh_ref[0]).astype(BF16)


def _modulate(x2, sc, sh, seq):
    m, d = x2.shape
    per_batch = seq // MOD_TR
    vec = pl.BlockSpec((1, 1, d), lambda i: (i // per_batch, 0, 0))
    return pl.pallas_call(
        _modulate_kernel,
        grid=(m // MOD_TR,),
        in_specs=[pl.BlockSpec((MOD_TR, d), lambda i: (i, 0)), vec, vec],
        out_specs=pl.BlockSpec((MOD_TR, d), lambda i: (i, 0)),
        out_shape=jax.ShapeDtypeStruct((m, d), BF16),
        compiler_params=_params(("arbitrary",), V7X_VMEM_BUDGET),
        name="modulate",
    )(x2, sc, sh)


ROPE_TR = 1024


def _rope_kernel(pos_ref, inv_ref, cos_ref, sin_ref):
    ang = pos_ref[...] * inv_ref[...]
    cos_ref[...] = jnp.cos(ang)
    sin_ref[...] = jnp.sin(ang)


def _rope_tables(pos_col, inv_freq):
    m = pos_col.shape[0]
    half = inv_freq.shape[1]
    out = jax.ShapeDtypeStruct((m, half), F32)
    return pl.pallas_call(
        _rope_kernel,
        grid=(m // ROPE_TR,),
        in_specs=[pl.BlockSpec((ROPE_TR, 1), lambda i: (i, 0)),
                  pl.BlockSpec((1, half), lambda i: (0, 0))],
        out_specs=[pl.BlockSpec((ROPE_TR, half), lambda i: (i, 0))] * 2,
        out_shape=[out, out],
        compiler_params=_params(("arbitrary",), 32 * MIB),
        name="rope",
    )(pos_col, inv_freq)


Z_TM = 1024
Z_TN = 1024
Z_RC = 256


def _zproj_kernel(*refs, nt, mt, bounds, hosted):
    nh = len(hosted)
    h_ref, w_ref, cos_ref, sin_ref = refs[:4]
    hosted_src = refs[4:4 + nh]
    o_ref = refs[4 + nh]
    hosted_dst = refs[5 + nh:5 + 2 * nh]
    wbf_ref = refs[5 + 2 * nh]
    n = pl.program_id(0)
    m = pl.program_id(1)
    half = HEAD_DIM // 2
    kc = w_ref.shape[0]
    wcur_ref = wbf_ref.at[(n + 1) % 2]
    wnext_ref = wbf_ref.at[n % 2]

    def stage_and_host():
        wnext_ref[pl.ds(pl.multiple_of(m * kc, kc), kc), :] = w_ref[...].astype(BF16)
        for cast, src, dst in zip(hosted, hosted_src, hosted_dst):
            cast.run(src, dst)

    @pl.when(n == 0)
    def _no_tile_yet():
        stage_and_host()

    t = n - 1

    def run(epilogue):
        stage_and_host()
        for r in range(Z_TM // Z_RC):
            rows = slice(r * Z_RC, (r + 1) * Z_RC)
            acc = jnp.dot(h_ref[rows, :], wcur_ref[...], preferred_element_type=F32)
            epilogue(acc, rows)

    def rotary(acc, rows):
        scale = jnp.where(t < q_end, 1.0, HEAD_DIM ** -0.5).astype(F32)
        c = cos_ref[rows, :]
        s = sin_ref[rows, :]
        for j in range(Z_TN // HEAD_DIM):
            x1 = acc[:, j * HEAD_DIM: j * HEAD_DIM + half]
            x2 = acc[:, j * HEAD_DIM + half: (j + 1) * HEAD_DIM]
            o_ref[rows, j * HEAD_DIM: j * HEAD_DIM + half] = ((x1 * c - x2 * s) * scale).astype(BF16)
            o_ref[rows, j * HEAD_DIM + half: (j + 1) * HEAD_DIM] = ((x2 * c + x1 * s) * scale).astype(BF16)

    def pointwise(fn):
        def epi(acc, rows):
            o_ref[rows, :] = fn(acc).astype(BF16)
        return epi

    q_end, k_end, v_end, g_end, us_end = bounds
    pl.when((n >= 1) & (t < k_end))(lambda: run(rotary))
    pl.when((t >= k_end) & (t < g_end))(
        lambda: run(pointwise(lambda a: jnp.where(t < v_end, a, jax.nn.silu(a)))))
    pl.when((t >= g_end) & (t < us_end))(lambda: run(pointwise(jax.nn.gelu)))
    pl.when(t >= us_end)(lambda: run(pointwise(jax.nn.sigmoid)))


def _zproj(h1, w_in, cos, sin, hosted_weights):
    m, d = h1.shape
    n = w_in.shape[1]
    nt, mt = n // Z_TN, m // Z_TM
    kc = d // mt
    ret_w = RET_HEADS * HEAD_DIM
    sg_w = SG_GROUPS * SG_GROUP_DIM
    bounds = tuple(b // Z_TN for b in (ret_w, 2 * ret_w, 3 * ret_w, 4 * ret_w, 4 * ret_w + 2 * sg_w))
    half = cos.shape[1]
    hosted = [_HostedCast(w, rb, first, lambda j, i: j * mt + i) for w, rb, first in hosted_weights]
    assert all(c.first_step + c.num_blocks <= (nt + 1) * mt for c in hosted)

    def row_tile(j, i):
        return (jnp.where(j == 0, 0, i), 0)

    def out_tile(j, i):
        return (jnp.where(j == 0, 0, i), jnp.maximum(j - 1, 0))

    outs = pl.pallas_call(
        functools.partial(_zproj_kernel, nt=nt, mt=mt, bounds=bounds, hosted=hosted),
        grid=(nt + 1, mt),
        in_specs=[pl.BlockSpec((Z_TM, d), row_tile),
                  pl.BlockSpec((kc, Z_TN), lambda j, i: (i, jnp.minimum(j, nt - 1))),
                  pl.BlockSpec((Z_TM, half), row_tile),
                  pl.BlockSpec((Z_TM, half), row_tile)] + [c.spec for c in hosted],
        out_specs=[pl.BlockSpec((Z_TM, Z_TN), out_tile)]
                  + [c.spec for c in hosted],
        out_shape=[jax.ShapeDtypeStruct((m, n), BF16)] + [c.out_shape for c in hosted],
        scratch_shapes=[pltpu.VMEM((2, d, Z_TN), BF16)],
        compiler_params=_params(("arbitrary", "arbitrary"), V7X_VMEM_BUDGET),
        name="zproj",
    )(h1, w_in, cos, sin, *[c.weight for c in hosted])
    return outs[0], outs[1:]


def _retention_chunk(cd_ref, q_ref, k_ref, v_ref, g_ref, dec_ref, xi_ref, zeta_ref, o_ref, state_ref):
    for h in range(RET_HEADS):
        cols = slice(h * HEAD_DIM, (h + 1) * HEAD_DIM)
        q = q_ref[:, cols]
        k = k_ref[:, cols]
        v = v_ref[:, cols]
        scores = lax.dot_general(q, k, (((1,), (1,)), ((), ())), preferred_element_type=F32)
        scores = scores * dec_ref[h]
        inner = jnp.dot(scores.astype(BF16), v, preferred_element_type=F32)
        state = state_ref[h]
        cross = jnp.dot(q, state.astype(BF16), preferred_element_type=F32) * xi_ref[h]
        kz = (k.astype(F32) * zeta_ref[h]).astype(BF16)
        update = lax.dot_general(kz, v, (((0,), (0,)), ((), ())), preferred_element_type=F32)
        state_ref[h] = cd_ref[h] * state + update
        y = inner + cross
        mu = jnp.mean(y, axis=-1, keepdims=True)
        yc = y - mu
        var = jnp.mean(yc * yc, axis=-1, keepdims=True)
        yn = yc * lax.rsqrt(var + LN_EPS)
        o_ref[:, cols] = (yn * g_ref[:, cols].astype(F32)).astype(BF16)


def _sgu_kernel(u_ref, s_ref, lng_ref, lnb_ref, w_ref, b_ref, o_ref):
    s = s_ref[...].astype(F32)
    mu = jnp.mean(s, axis=-1, keepdims=True)
    sc = s - mu
    var = jnp.mean(sc * sc, axis=-1, keepdims=True)
    sn = (sc * lax.rsqrt(var + LN_EPS) * lng_ref[...] + lnb_ref[...]).astype(BF16)
    t_idx = lax.broadcasted_iota(jnp.int32, (CHUNK, CHUNK), 0)
    s_idx = lax.broadcasted_iota(jnp.int32, (CHUNK, CHUNK), 1)
    causal = t_idx >= s_idx
    for g in range(SG_GROUPS):
        cols = slice(g * SG_GROUP_DIM, (g + 1) * SG_GROUP_DIM)
        w = jnp.where(causal, w_ref[g], 0.0).astype(BF16)
        mixed = jnp.dot(w, sn[:, cols], preferred_element_type=F32) + b_ref[g]
        o_ref[:, cols] = (u_ref[:, cols].astype(F32) * mixed).astype(BF16)


def _mixers_kernel(cd_ref, q_ref, k_ref, v_ref, g_ref, dec_ref, xi_ref, zeta_ref,
                   u_ref, s_ref, lng_ref, lnb_ref, w_ref, b_ref, ret_ref, sgu_ref, state_ref):
    @pl.when(pl.program_id(0) == 0)
    def _():
        state_ref[...] = jnp.zeros_like(state_ref)

    for b in range(q_ref.shape[0]):
        _retention_chunk(cd_ref, q_ref.at[b], k_ref.at[b], v_ref.at[b], g_ref.at[b], dec_ref, xi_ref,
                         zeta_ref, ret_ref.at[b], state_ref.at[b])
        _sgu_kernel(u_ref.at[b], s_ref.at[b], lng_ref, lnb_ref, w_ref, b_ref, sgu_ref.at[b])


def _mixers(z, chunk_decay, inner_decay, xi_b, zeta_b, ln_g, ln_b, w_s, b_col, batch, seq):
    m, z_cols = z.shape
    width = RET_HEADS * HEAD_DIM
    assert width == SG_GROUPS * SG_GROUP_DIM
    z3 = z.reshape(batch, seq, z_cols)

    def zblock(col):
        return pl.BlockSpec((batch, CHUNK, width), lambda c: (0, c, col))

    def const(shape):
        return pl.BlockSpec(shape, lambda c: (0,) * len(shape))

    out_spec = pl.BlockSpec((batch, CHUNK, width), lambda c: (0, c, 0))
    out_shape = jax.ShapeDtypeStruct((batch, seq, width), BF16)
    ret, sgu = pl.pallas_call(
        _mixers_kernel,
        grid=(seq // CHUNK,),
        in_specs=[pl.BlockSpec(memory_space=pltpu.SMEM),
                  zblock(0), zblock(1), zblock(2), zblock(3),
                  const(inner_decay.shape), const(xi_b.shape), const(zeta_b.shape),
                  zblock(4), zblock(5), const(ln_g.shape), const(ln_b.shape),
                  const(w_s.shape), const(b_col.shape)],
        out_specs=[out_spec, out_spec],
        out_shape=[out_shape, out_shape],
        scratch_shapes=[pltpu.VMEM((batch, RET_HEADS, HEAD_DIM, HEAD_DIM), F32)],
        compiler_params=_params(("arbitrary",), 32 * MIB),
        name="mixers",
    )(chunk_decay, z3, z3, z3, z3, inner_decay, xi_b, zeta_b, z3, z3, ln_g, ln_b, w_s, b_col)
    return ret.reshape(m, width), sgu.reshape(m, width)


MIX_TM = 1024
MIX_TN = 1024
MIX_RC = 256


def _mix_kernel(ret_ref, sgu_ref, wr_ref, ws_ref, gr_ref, gs_ref, o_ref):
    for r in range(MIX_TM // MIX_RC):
        rows = slice(r * MIX_RC, (r + 1) * MIX_RC)
        y_ret = jnp.dot(ret_ref[rows, :], wr_ref[...], preferred_element_type=F32)
        y_sg = jnp.dot(sgu_ref[rows, :], ws_ref[...], preferred_element_type=F32)
        mix = gr_ref[rows, :].astype(F32) * y_ret + gs_ref[rows, :].astype(F32) * y_sg
        o_ref[rows, :] = mix.astype(BF16)


def _mix(ret, sgu, w_ret_up, w_sg_up, z, gate_col0):
    m, kdim = ret.shape
    n = w_ret_up.shape[1]
    nt = n // MIX_TN
    g0 = gate_col0 // MIX_TN
    g1 = g0 + nt
    return pl.pallas_call(
        _mix_kernel,
        grid=(m // MIX_TM, nt),
        in_specs=[pl.BlockSpec((MIX_TM, kdim), lambda i, j: (i, 0)),
                  pl.BlockSpec((MIX_TM, kdim), lambda i, j: (i, 0)),
                  pl.BlockSpec((kdim, MIX_TN), lambda i, j: (0, j)),
                  pl.BlockSpec((kdim, MIX_TN), lambda i, j: (0, j)),
                  pl.BlockSpec((MIX_TM, MIX_TN), lambda i, j: (i, g0 + j)),
                  pl.BlockSpec((MIX_TM, MIX_TN), lambda i, j: (i, g1 + j))],
        out_specs=pl.BlockSpec((MIX_TM, MIX_TN), lambda i, j: (i, j)),
        out_shape=jax.ShapeDtypeStruct((m, n), BF16),
        compiler_params=_params(("arbitrary", "arbitrary"), V7X_VMEM_BUDGET),
        name="mix",
    )(ret, sgu, w_ret_up, w_sg_up, z, z)


OUT_TM = 1024
OUT_TN = 1024
OUT_RC = 256


def _outproj_kernel(a_ref, w_ref, c_ref, wa_ref, ba_ref, o_ref, ada_ref):
    ada_ref[...] = _ada_columns(c_ref, wa_ref, ba_ref)
    for r in range(OUT_TM // OUT_RC):
        rows = slice(r * OUT_RC, (r + 1) * OUT_RC)
        o_ref[rows, :] = jnp.dot(a_ref[rows, :], w_ref[...], preferred_element_type=F32).astype(BF16)


def _outproj(a, w, c_pad, w_ada, b_ada, ada_col0):
    m, kdim = a.shape
    n = w.shape[1]
    nt = n // OUT_TN
    steps = (m // OUT_TM) * nt
    assert ada_col0 % ADA_TN == 0 and ada_col0 + steps * ADA_TN == w_ada.shape[1]
    block0 = ada_col0 // ADA_TN
    rows, d = c_pad.shape
    return pl.pallas_call(
        _outproj_kernel,
        grid=(m // OUT_TM, nt),
        in_specs=[pl.BlockSpec((OUT_TM, kdim), lambda i, j: (i, 0)),
                  pl.BlockSpec((kdim, OUT_TN), lambda i, j: (0, j)),
                  pl.BlockSpec((rows, d), lambda i, j: (0, 0)),
                  pl.BlockSpec((d, ADA_TN), lambda i, j: (0, block0 + i * nt + j)),
                  pl.BlockSpec((1, ADA_TN), lambda i, j: (0, block0 + i * nt + j))],
        out_specs=[pl.BlockSpec((OUT_TM, OUT_TN), lambda i, j: (i, j)),
                   pl.BlockSpec((rows, ADA_TN), lambda i, j: (0, i * nt + j))],
        out_shape=[jax.ShapeDtypeStruct((m, n), BF16),
                   jax.ShapeDtypeStruct((rows, steps * ADA_TN), F32)],
        compiler_params=_params(("arbitrary", "arbitrary"), V7X_VMEM_BUDGET),
        name="outproj",
    )(a, w, c_pad, w_ada, b_ada)


LN_TR = 256


def _residual_ln(x, branch, gate, ln_g, ln_b):
    y = ALPHA * x + (1.0 + gate) * branch.astype(F32)
    mu = jnp.mean(y, axis=-1, keepdims=True)
    yc = y - mu
    var = jnp.mean(yc * yc, axis=-1, keepdims=True)
    return yc * lax.rsqrt(var + LN_EPS) * ln_g + ln_b


def _ln1_kernel(x_ref, br_ref, gate_ref, lng_ref, lnb_ref, sc_ref, sh_ref, x1_ref, h2_ref):
    x1 = _residual_ln(x_ref[...], br_ref[...], gate_ref[0], lng_ref[...], lnb_ref[...])
    x1_ref[...] = x1
    h2_ref[...] = (x1 * (1.0 + sc_ref[0]) + sh_ref[0]).astype(BF16)


def _ln2_kernel(x_ref, br_ref, gate_ref, lng_ref, lnb_ref, o_ref):
    o_ref[...] = _residual_ln(x_ref[...], br_ref[...], gate_ref[0], lng_ref[...], lnb_ref[...])


def _ln_call(kernel, name, x, branch, batch_vecs, row_vecs, out_dtypes, seq):
    m, d = x.shape
    per_batch = seq // LN_TR
    tile = pl.BlockSpec((LN_TR, d), lambda i: (i, 0))
    bvec = pl.BlockSpec((1, 1, d), lambda i: (i // per_batch, 0, 0))
    rvec = pl.BlockSpec((1, d), lambda i: (0, 0))
    operands = [x, branch, batch_vecs[0], *row_vecs, *batch_vecs[1:]]
    specs = [tile, tile, bvec] + [rvec] * len(row_vecs) + [bvec] * (len(batch_vecs) - 1)
    return pl.pallas_call(
        kernel,
        grid=(m // LN_TR,),
        in_specs=specs,
        out_specs=[tile] * len(out_dtypes),
        out_shape=[jax.ShapeDtypeStruct((m, d), dt) for dt in out_dtypes],
        compiler_params=_params(("arbitrary",), V7X_VMEM_BUDGET),
        name=name,
    )(*operands)


UP_TM = 4096
UP_TF = 256
UP_RC = 512
DOWN_TM = 512
DOWN_TN = 1024
DOWN_RC = 256
RESIDENT = pl.Buffered(1)


def _ffn_up_kernel(h_ref, wg_ref, wu_ref, cast_src, o_ref, cast_dst):
    _HostedCast.run(cast_src, cast_dst)
    for r in range(UP_TM // UP_RC):
        rows = slice(r * UP_RC, (r + 1) * UP_RC)
        h = h_ref[rows, :]
        a_gate = jnp.dot(h, wg_ref[...], preferred_element_type=F32)
        a_up = jnp.dot(h, wu_ref[...], preferred_element_type=F32)
        o_ref[rows, :] = (jax.nn.silu(a_gate) * a_up).astype(BF16)


def _ffn_up(h2, w_gate_up, hosted_weight, hosted_rows):
    m, d = h2.shape
    d_ff = w_gate_up.shape[1] // 2
    nf = d_ff // UP_TF
    hosted = _HostedCast(hosted_weight, hosted_rows, 0, lambda i, f: i * nf + f)
    assert hosted.num_blocks <= (m // UP_TM) * nf
    return pl.pallas_call(
        _ffn_up_kernel,
        grid=(m // UP_TM, nf),
        in_specs=[pl.BlockSpec((UP_TM, d), lambda i, f: (i, 0), pipeline_mode=RESIDENT),
                  pl.BlockSpec((d, UP_TF), lambda i, f: (0, f)),
                  pl.BlockSpec((d, UP_TF), lambda i, f: (0, nf + f)),
                  hosted.spec],
        out_specs=[pl.BlockSpec((UP_TM, UP_TF), lambda i, f: (i, f)), hosted.spec],
        out_shape=[jax.ShapeDtypeStruct((m, d_ff), BF16), hosted.out_shape],
        compiler_params=_params(("arbitrary", "arbitrary"), V7X_VMEM_BUDGET),
        name="ffn_up",
    )(h2, w_gate_up, w_gate_up, hosted_weight)


def _ffn_down_kernel(a_ref, w_ref, o_ref):
    for r in range(DOWN_TM // DOWN_RC):
        rows = slice(r * DOWN_RC, (r + 1) * DOWN_RC)
        o_ref[rows, :] = jnp.dot(a_ref[rows, :], w_ref[...], preferred_element_type=F32).astype(BF16)


def _ffn_down(act, w_down):
    m, d_ff = act.shape
    d = w_down.shape[1]
    return pl.pallas_call(
        _ffn_down_kernel,
        grid=(d // DOWN_TN, m // DOWN_TM),
        in_specs=[pl.BlockSpec((DOWN_TM, d_ff), lambda j, i: (i, 0)),
                  pl.BlockSpec((d_ff, DOWN_TN), lambda j, i: (0, j), pipeline_mode=RESIDENT)],
        out_specs=pl.BlockSpec((DOWN_TM, DOWN_TN), lambda j, i: (i, j)),
        out_shape=jax.ShapeDtypeStruct((m, d), BF16),
        compiler_params=_params(("arbitrary", "arbitrary"), V7X_VMEM_BUDGET),
        name="ffn_down",
    )(act, w_down)


def _decay_tables():
    log_g = jnp.log(1.0 - 2.0 ** (-5.0 - jnp.arange(RET_HEADS, dtype=F32)))
    idx = jnp.arange(CHUNK, dtype=F32)
    diff = idx[:, None] - idx[None, :]
    inner = jnp.where((diff >= 0)[None],
                      jnp.exp(log_g[:, None, None] * jnp.maximum(diff, 0.0)[None]), 0.0)
    xi = jnp.exp(log_g[:, None] * (idx[None, :] + 1.0))
    zeta = jnp.exp(log_g[:, None] * (CHUNK - 1.0 - idx[None, :]))
    chunk_decay = jnp.exp(log_g * CHUNK)
    xi_b = jnp.broadcast_to(xi[:, :, None], (RET_HEADS, CHUNK, HEAD_DIM))
    zeta_b = jnp.broadcast_to(zeta[:, :, None], (RET_HEADS, CHUNK, HEAD_DIM))
    return chunk_decay, inner, xi_b, zeta_b


def kernel(x, c, positions, w_ada, b_ada, w_in, sg_ln_g, sg_ln_b, sg_w, sg_b, w_ret_up, w_sg_up,
           w_out, ln1_g, ln1_b, w_gate_up, w_down, ln2_g, ln2_b):
    batch, seq, d = x.shape
    m = batch * seq
    assert w_ada.shape[0] == DEPTH == 1
    x2 = x.reshape(m, d)

    c_pad = jnp.zeros((8, d), F32).at[:batch].set(c)
    b_ada_row = b_ada[0][None, :]
    sh1, sc1 = [t[:, None, :] for t in jnp.split(_ada(c_pad, w_ada[0], b_ada_row, 2 * d)[:batch], 2, axis=-1)]

    h1 = _modulate(x2, sc1, sh1, seq)

    inv_freq = (ROPE_BASE ** (-jnp.arange(0, HEAD_DIM, 2, dtype=F32) / HEAD_DIM))[None, :]
    cos, sin = _rope_tables(positions.astype(F32).reshape(m, 1), inv_freq)

    z, (w_gate_up_bf, w_out_bf, w_ret_up_bf, w_sg_up_bf) = _zproj(
        h1, w_in[0], cos, sin,
        [(w_gate_up[0], 32, 0), (w_out[0], 32, 0), (w_ret_up[0], 16, 40), (w_sg_up[0], 16, 40)])

    chunk_decay, inner_decay, xi_b, zeta_b = _decay_tables()
    ret, sgu = _mixers(z, chunk_decay, inner_decay, xi_b, zeta_b, sg_ln_g[0][None, :],
                       sg_ln_b[0][None, :], sg_w[0], sg_b[0][:, :, None], batch, seq)

    gate_col0 = 4 * RET_HEADS * HEAD_DIM + 2 * SG_GROUPS * SG_GROUP_DIM
    mix = _mix(ret, sgu, w_ret_up_bf, w_sg_up_bf, z, gate_col0)
    branch1, ada_late = _outproj(mix, w_out_bf, c_pad, w_ada[0], b_ada_row, 2 * d)
    g1, sh2, sc2, g2 = [t[:, None, :] for t in jnp.split(ada_late[:batch], 4, axis=-1)]
    x1, h2 = _ln_call(_ln1_kernel, "ln1", x2, branch1, [g1, sc2, sh2],
                      [ln1_g[0][None, :], ln1_b[0][None, :]], [F32, BF16], seq)

    act, w_down_bf = _ffn_up(h2, w_gate_up_bf, w_down[0], 128)
    ffn = _ffn_down(act, w_down_bf)
    (out,) = _ln_call(_ln2_kernel, "ln2", x1, ffn, [g2],
                      [ln2_g[0][None, :], ln2_b[0][None, :]], [F32], seq)
    return out.reshape(batch, seq, d)
```

```python
import functools

import jax
import jax.numpy as jnp
from jax import lax
from jax.experimental import pallas as pl
from jax.experimental.pallas import tpu as pltpu

F32 = jnp.float32
BF16 = jnp.bfloat16

RET_HEADS = 8
HEAD_DIM = 256
SG_GROUPS = 8
SG_GROUP_DIM = 256
CHUNK = 128
ROPE_BASE = 10000.0
LN_EPS = 1e-5
DEPTH = 1
ALPHA = (2.0 * DEPTH) ** 0.25

MIB = 1024 * 1024
V7X_VMEM_BUDGET = 56 * MIB


def _params(sem, vmem_bytes):
    return pltpu.CompilerParams(dimension_semantics=sem, vmem_limit_bytes=vmem_bytes)


class _HostedCast:
    def __init__(self, weight, rows_per_block, first_step, step_of):
        rows, cols = weight.shape
        self.weight = weight
        self.first_step = first_step
        self.num_blocks = rows // rows_per_block
        assert self.num_blocks * rows_per_block == rows
        last = self.num_blocks - 1

        def index_map(*grid_ids):
            return (jnp.clip(step_of(*grid_ids) - first_step, 0, last), 0)

        self.spec = pl.BlockSpec((rows_per_block, cols), index_map)
        self.out_shape = jax.ShapeDtypeStruct((rows, cols), BF16)

    @staticmethod
    def run(src_ref, dst_ref):
        dst_ref[...] = src_ref[...].astype(BF16)


ADA_TN = 512


def _ada_columns(c_ref, w_ref, b_ref):
    s = jax.nn.silu(c_ref[...]).astype(BF16)
    return jnp.dot(s, w_ref[...].astype(BF16), preferred_element_type=F32) + b_ref[...]


def _ada_kernel(c_ref, w_ref, b_ref, o_ref):
    o_ref[...] = _ada_columns(c_ref, w_ref, b_ref)


def _ada(c_pad, w_ada, b_ada, n):
    rows, d = c_pad.shape
    return pl.pallas_call(
        _ada_kernel,
        grid=(n // ADA_TN,),
        in_specs=[pl.BlockSpec((rows, d), lambda j: (0, 0)),
                  pl.BlockSpec((d, ADA_TN), lambda j: (0, j)),
                  pl.BlockSpec((1, ADA_TN), lambda j: (0, j))],
        out_specs=pl.BlockSpec((rows, ADA_TN), lambda j: (0, j)),
        out_shape=jax.ShapeDtypeStruct((rows, n), F32),
        compiler_params=_params(("arbitrary",), 32 * MIB),
        name="ada",
    )(c_pad, w_ada, b_ada)


MOD_TR = 1024


def _modulate_kernel(x_ref, sc_ref, sh_ref, o_ref):
    o_ref[...] = (x_ref[...] * (1.0 + sc_ref[0]) + sh_ref[0]).astype(BF16)


def _modulate(x2, sc, sh, seq):
    m, d = x2.shape
    per_batch = seq // MOD_TR
    vec = pl.BlockSpec((1, 1, d), lambda i: (i // per_batch, 0, 0))
    return pl.pallas_call(
        _modulate_kernel,
        grid=(m // MOD_TR,),
        in_specs=[pl.BlockSpec((MOD_TR, d), lambda i: (i, 0)), vec, vec],
        out_specs=pl.BlockSpec((MOD_TR, d), lambda i: (i, 0)),
        out_shape=jax.ShapeDtypeStruct((m, d), BF16),
        compiler_params=_params(("arbitrary",), V7X_VMEM_BUDGET),
        name="modulate",
    )(x2, sc, sh)


ROPE_TR = 1024


def _rope_kernel(pos_ref, inv_ref, cos_ref, sin_ref):
    ang = pos_ref[...] * inv_ref[...]
    cos_ref[...] = jnp.cos(ang)
    sin_ref[...] = jnp.sin(ang)


def _rope_tables(pos_col, inv_freq):
    m = pos_col.shape[0]
    half = inv_freq.shape[1]
    out = jax.ShapeDtypeStruct((m, half), F32)
    return pl.pallas_call(
        _rope_kernel,
        grid=(m // ROPE_TR,),
        in_specs=[pl.BlockSpec((ROPE_TR, 1), lambda i: (i, 0)),
                  pl.BlockSpec((1, half), lambda i: (0, 0))],
        out_specs=[pl.BlockSpec((ROPE_TR, half), lambda i: (i, 0))] * 2,
        out_shape=[out, out],
        compiler_params=_params(("arbitrary",), 32 * MIB),
        name="rope",
    )(pos_col, inv_freq)


Z_TM = 1024
Z_TN = 1024
Z_RC = 256


def _zproj_kernel(*refs, nt, mt, bounds, hosted):
    nh = len(hosted)
    h_ref, w_ref, cos_ref, sin_ref = refs[:4]
    hosted_src = refs[4:4 + nh]
    o_ref = refs[4 + nh]
    hosted_dst = refs[5 + nh:5 + 2 * nh]
    wbf_ref = refs[5 + 2 * nh]
    n = pl.program_id(0)
    m = pl.program_id(1)
    half = HEAD_DIM // 2
    kc = w_ref.shape[0]
    wcur_ref = wbf_ref.at[(n + 1) % 2]
    wnext_ref = wbf_ref.at[n % 2]

    def stage_and_host():
        wnext_ref[pl.ds(pl.multiple_of(m * kc, kc), kc), :] = w_ref[...].astype(BF16)
        for cast, src, dst in zip(hosted, hosted_src, hosted_dst):
            cast.run(src, dst)

    @pl.when(n == 0)
    def _no_tile_yet():
        stage_and_host()

    t = n - 1

    def run(epilogue):
        stage_and_host()
        for r in range(Z_TM // Z_RC):
            rows = slice(r * Z_RC, (r + 1) * Z_RC)
            acc = jnp.dot(h_ref[rows, :], wcur_ref[...], preferred_element_type=F32)
            epilogue(acc, rows)

    def rotary(acc, rows):
        scale = jnp.where(t < q_end, 1.0, HEAD_DIM ** -0.5).astype(F32)
        c = cos_ref[rows, :]
        s = sin_ref[rows, :]
        for j in range(Z_TN // HEAD_DIM):
            x1 = acc[:, j * HEAD_DIM: j * HEAD_DIM + half]
            x2 = acc[:, j * HEAD_DIM + half: (j + 1) * HEAD_DIM]
            o_ref[rows, j * HEAD_DIM: j * HEAD_DIM + half] = ((x1 * c - x2 * s) * scale).astype(BF16)
            o_ref[rows, j * HEAD_DIM + half: (j + 1) * HEAD_DIM] = ((x2 * c + x1 * s) * scale).astype(BF16)

    def pointwise(fn):
        def epi(acc, rows):
            o_ref[rows, :] = fn(acc).astype(BF16)
        return epi

    q_end, k_end, v_end, g_end, us_end = bounds
    pl.when((n >= 1) & (t < k_end))(lambda: run(rotary))
    pl.when((t >= k_end) & (t < g_end))(
        lambda: run(pointwise(lambda a: jnp.where(t < v_end, a, jax.nn.silu(a)))))
    pl.when((t >= g_end) & (t < us_end))(lambda: run(pointwise(jax.nn.gelu)))
    pl.when(t >= us_end)(lambda: run(pointwise(jax.nn.sigmoid)))


def _zproj(h1, w_in, cos, sin, hosted_weights):
    m, d = h1.shape
    n = w_in.shape[1]
    nt, mt = n // Z_TN, m // Z_TM
    kc = d // mt
    ret_w = RET_HEADS * HEAD_DIM
    sg_w = SG_GROUPS * SG_GROUP_DIM
    bounds = tuple(b // Z_TN for b in (ret_w, 2 * ret_w, 3 * ret_w, 4 * ret_w, 4 * ret_w + 2 * sg_w))
    half = cos.shape[1]
    hosted = [_HostedCast(w, rb, first, lambda j, i: j * mt + i) for w, rb, first in hosted_weights]
    assert all(c.first_step + c.num_blocks <= (nt + 1) * mt for c in hosted)

    def row_tile(j, i):
        return (jnp.where(j == 0, 0, i), 0)

    def out_tile(j, i):
        return (jnp.where(j == 0, 0, i), jnp.maximum(j - 1, 0))

    outs = pl.pallas_call(
        functools.partial(_zproj_kernel, nt=nt, mt=mt, bounds=bounds, hosted=hosted),
        grid=(nt + 1, mt),
        in_specs=[pl.BlockSpec((Z_TM, d), row_tile),
                  pl.BlockSpec((kc, Z_TN), lambda j, i: (i, jnp.minimum(j, nt - 1))),
                  pl.BlockSpec((Z_TM, half), row_tile),
                  pl.BlockSpec((Z_TM, half), row_tile)] + [c.spec for c in hosted],
        out_specs=[pl.BlockSpec((Z_TM, Z_TN), out_tile)]
                  + [c.spec for c in hosted],
        out_shape=[jax.ShapeDtypeStruct((m, n), BF16)] + [c.out_shape for c in hosted],
        scratch_shapes=[pltpu.VMEM((2, d, Z_TN), BF16)],
        compiler_params=_params(("arbitrary", "arbitrary"), V7X_VMEM_BUDGET),
        name="zproj",
    )(h1, w_in, cos, sin, *[c.weight for c in hosted])
    return outs[0], outs[1:]


def _retention_chunk(cd_ref, q_ref, k_ref, v_ref, g_ref, dec_ref, xi_ref, zeta_ref, o_ref, state_ref):
    for h in range(RET_HEADS):
        cols = slice(h * HEAD_DIM, (h + 1) * HEAD_DIM)
        q = q_ref[:, cols]
        k = k_ref[:, cols]
        v = v_ref[:, cols]
        scores = lax.dot_general(q, k, (((1,), (1,)), ((), ())), preferred_element_type=F32)
        scores = scores * dec_ref[h]
        inner = jnp.dot(scores.astype(BF16), v, preferred_element_type=F32)
        state = state_ref[h]
        cross = jnp.dot(q, state.astype(BF16), preferred_element_type=F32) * xi_ref[h]
        kz = (k.astype(F32) * zeta_ref[h]).astype(BF16)
        update = lax.dot_general(kz, v, (((0,), (0,)), ((), ())), preferred_element_type=F32)
        state_ref[h] = cd_ref[h] * state + update
        y = inner + cross
        mu = jnp.mean(y, axis=-1, keepdims=True)
        yc = y - mu
        var = jnp.mean(yc * yc, axis=-1, keepdims=True)
        yn = yc * lax.rsqrt(var + LN_EPS)
        o_ref[:, cols] = yn.astype(BF16) * g_ref[:, cols]


def _sgu_kernel(u_ref, s_ref, lng_ref, lnb_ref, w_ref, b_ref, o_ref):
    s = s_ref[...].astype(F32)
    mu = jnp.mean(s, axis=-1, keepdims=True)
    sc = s - mu
    var = jnp.mean(sc * sc, axis=-1, keepdims=True)
    sn = (sc * lax.rsqrt(var + LN_EPS) * lng_ref[...] + lnb_ref[...]).astype(BF16)
    t_idx = lax.broadcasted_iota(jnp.int32, (CHUNK, CHUNK), 0)
    s_idx = lax.broadcasted_iota(jnp.int32, (CHUNK, CHUNK), 1)
    causal = t_idx >= s_idx
    for g in range(SG_GROUPS):
        cols = slice(g * SG_GROUP_DIM, (g + 1) * SG_GROUP_DIM)
        w = jnp.where(causal, w_ref[g], 0.0).astype(BF16)
        mixed = jnp.dot(w, sn[:, cols], preferred_element_type=F32) + b_ref[g]
        o_ref[:, cols] = u_ref[:, cols] * mixed.astype(BF16)


def _mixers_kernel(cd_ref, q_ref, k_ref, v_ref, g_ref, dec_ref, xi_ref, zeta_ref,
                   u_ref, s_ref, lng_ref, lnb_ref, w_ref, b_ref, ret_ref, sgu_ref, state_ref):
    @pl.when(pl.program_id(0) == 0)
    def _():
        state_ref[...] = jnp.zeros_like(state_ref)

    for b in range(q_ref.shape[0]):
        _retention_chunk(cd_ref, q_ref.at[b], k_ref.at[b], v_ref.at[b], g_ref.at[b], dec_ref, xi_ref,
                         zeta_ref, ret_ref.at[b], state_ref.at[b])
        _sgu_kernel(u_ref.at[b], s_ref.at[b], lng_ref, lnb_ref, w_ref, b_ref, sgu_ref.at[b])


def _mixers(z, chunk_decay, inner_decay, xi_b, zeta_b, ln_g, ln_b, w_s, b_col, batch, seq):
    m, z_cols = z.shape
    width = RET_HEADS * HEAD_DIM
    assert width == SG_GROUPS * SG_GROUP_DIM
    z3 = z.reshape(batch, seq, z_cols)

    def zblock(col):
        return pl.BlockSpec((batch, CHUNK, width), lambda c: (0, c, col))

    def const(shape):
        return pl.BlockSpec(shape, lambda c: (0,) * len(shape))

    out_spec = pl.BlockSpec((batch, CHUNK, width), lambda c: (0, c, 0))
    out_shape = jax.ShapeDtypeStruct((batch, seq, width), BF16)
    ret, sgu = pl.pallas_call(
        _mixers_kernel,
        grid=(seq // CHUNK,),
        in_specs=[pl.BlockSpec(memory_space=pltpu.SMEM),
                  zblock(0), zblock(1), zblock(2), zblock(3),
                  const(inner_decay.shape), const(xi_b.shape), const(zeta_b.shape),
                  zblock(4), zblock(5), const(ln_g.shape), const(ln_b.shape),
                  const(w_s.shape), const(b_col.shape)],
        out_specs=[out_spec, out_spec],
        out_shape=[out_shape, out_shape],
        scratch_shapes=[pltpu.VMEM((batch, RET_HEADS, HEAD_DIM, HEAD_DIM), F32)],
        compiler_params=_params(("arbitrary",), 32 * MIB),
        name="mixers",
    )(chunk_decay, z3, z3, z3, z3, inner_decay, xi_b, zeta_b, z3, z3, ln_g, ln_b, w_s, b_col)
    return ret.reshape(m, width), sgu.reshape(m, width)


MIX_TM = 1024
MIX_TN = 1024
MIX_RC = 256


def _mix_kernel(ret_ref, sgu_ref, wr_ref, ws_ref, gr_ref, gs_ref, o_ref):
    for r in range(MIX_TM // MIX_RC):
        rows = slice(r * MIX_RC, (r + 1) * MIX_RC)
        y_ret = jnp.dot(ret_ref[rows, :], wr_ref[...], preferred_element_type=F32)
        y_sg = jnp.dot(sgu_ref[rows, :], ws_ref[...], preferred_element_type=F32)
        mix = gr_ref[rows, :].astype(F32) * y_ret + gs_ref[rows, :].astype(F32) * y_sg
        o_ref[rows, :] = mix.astype(BF16)


def _mix(ret, sgu, w_ret_up, w_sg_up, z, gate_col0):
    m, kdim = ret.shape
    n = w_ret_up.shape[1]
    nt = n // MIX_TN
    g0 = gate_col0 // MIX_TN
    g1 = g0 + nt
    return pl.pallas_call(
        _mix_kernel,
        grid=(m // MIX_TM, nt),
        in_specs=[pl.BlockSpec((MIX_TM, kdim), lambda i, j: (i, 0)),
                  pl.BlockSpec((MIX_TM, kdim), lambda i, j: (i, 0)),
                  pl.BlockSpec((kdim, MIX_TN), lambda i, j: (0, j)),
                  pl.BlockSpec((kdim, MIX_TN), lambda i, j: (0, j)),
                  pl.BlockSpec((MIX_TM, MIX_TN), lambda i, j: (i, g0 + j)),
                  pl.BlockSpec((MIX_TM, MIX_TN), lambda i, j: (i, g1 + j))],
        out_specs=pl.BlockSpec((MIX_TM, MIX_TN), lambda i, j: (i, j)),
        out_shape=jax.ShapeDtypeStruct((m, n), BF16),
        compiler_params=_params(("arbitrary", "arbitrary"), V7X_VMEM_BUDGET),
        name="mix",
    )(ret, sgu, w_ret_up, w_sg_up, z, z)


OUT_TM = 1024
OUT_TN = 1024
OUT_RC = 256


def _outproj_kernel(a_ref, w_ref, c_ref, wa_ref, ba_ref, o_ref, ada_ref):
    ada_ref[...] = _ada_columns(c_ref, wa_ref, ba_ref)
    for r in range(OUT_TM // OUT_RC):
        rows = slice(r * OUT_RC, (r + 1) * OUT_RC)
        o_ref[rows, :] = jnp.dot(a_ref[rows, :], w_ref[...], preferred_element_type=F32).astype(BF16)


def _outproj(a, w, c_pad, w_ada, b_ada, ada_col0):
    m, kdim = a.shape
    n = w.shape[1]
    nt = n // OUT_TN
    steps = (m // OUT_TM) * nt
    assert ada_col0 % ADA_TN == 0 and ada_col0 + steps * ADA_TN == w_ada.shape[1]
    block0 = ada_col0 // ADA_TN
    rows, d = c_pad.shape
    return pl.pallas_call(
        _outproj_kernel,
        grid=(m // OUT_TM, nt),
        in_specs=[pl.BlockSpec((OUT_TM, kdim), lambda i, j: (i, 0)),
                  pl.BlockSpec((kdim, OUT_TN), lambda i, j: (0, j)),
                  pl.BlockSpec((rows, d), lambda i, j: (0, 0)),
                  pl.BlockSpec((d, ADA_TN), lambda i, j: (0, block0 + i * nt + j)),
                  pl.BlockSpec((1, ADA_TN), lambda i, j: (0, block0 + i * nt + j))],
        out_specs=[pl.BlockSpec((OUT_TM, OUT_TN), lambda i, j: (i, j)),
                   pl.BlockSpec((rows, ADA_TN), lambda i, j: (0, i * nt + j))],
        out_shape=[jax.ShapeDtypeStruct((m, n), BF16),
                   jax.ShapeDtypeStruct((rows, steps * ADA_TN), F32)],
        compiler_params=_params(("arbitrary", "arbitrary"), V7X_VMEM_BUDGET),
        name="outproj",
    )(a, w, c_pad, w_ada, b_ada)


LN_TR = 256


def _residual_ln(x, branch, gate, ln_g, ln_b):
    y = ALPHA * x + (1.0 + gate) * branch.astype(F32)
    mu = jnp.mean(y, axis=-1, keepdims=True)
    yc = y - mu
    var = jnp.mean(yc * yc, axis=-1, keepdims=True)
    return yc * lax.rsqrt(var + LN_EPS) * ln_g + ln_b


def _ln1_kernel(x_ref, br_ref, gate_ref, lng_ref, lnb_ref, sc_ref, sh_ref, h2_ref):
    x1 = _residual_ln(x_ref[...], br_ref[...], gate_ref[0], lng_ref[...], lnb_ref[...])
    h2_ref[...] = (x1 * (1.0 + sc_ref[0]) + sh_ref[0]).astype(BF16)


def _ln2_kernel(x_ref, br1_ref, gate1_ref, lng1_ref, lnb1_ref, br2_ref, gate2_ref, lng2_ref, lnb2_ref, o_ref):
    x1 = _residual_ln(x_ref[...], br1_ref[...], gate1_ref[0], lng1_ref[...], lnb1_ref[...])
    o_ref[...] = _residual_ln(x1, br2_ref[...], gate2_ref[0], lng2_ref[...], lnb2_ref[...])


def _ln_call(kernel, name, operands, kinds, out_dtype, seq):
    m, d = operands[0].shape
    per_batch = seq // LN_TR
    spec = {"t": pl.BlockSpec((LN_TR, d), lambda i: (i, 0)),
            "b": pl.BlockSpec((1, 1, d), lambda i: (i // per_batch, 0, 0)),
            "r": pl.BlockSpec((1, d), lambda i: (0, 0))}
    return pl.pallas_call(
        kernel,
        grid=(m // LN_TR,),
        in_specs=[spec[k] for k in kinds],
        out_specs=spec["t"],
        out_shape=jax.ShapeDtypeStruct((m, d), out_dtype),
        compiler_params=_params(("arbitrary",), V7X_VMEM_BUDGET),
        name=name,
    )(*operands)


UP_TM = 4096
UP_TF = 256
UP_RC = 512
DOWN_TM = 512
DOWN_TN = 1024
DOWN_RC = 256
RESIDENT = pl.Buffered(1)


def _ffn_up_kernel(h_ref, wg_ref, wu_ref, cast_src, o_ref, cast_dst):
    _HostedCast.run(cast_src, cast_dst)
    for r in range(UP_TM // UP_RC):
        rows = slice(r * UP_RC, (r + 1) * UP_RC)
        h = h_ref[rows, :]
        a_gate = jnp.dot(h, wg_ref[...], preferred_element_type=F32)
        a_up = jnp.dot(h, wu_ref[...], preferred_element_type=F32)
        o_ref[rows, :] = (jax.nn.silu(a_gate) * a_up).astype(BF16)


def _ffn_up(h2, w_gate_up, hosted_weight, hosted_rows):
    m, d = h2.shape
    d_ff = w_gate_up.shape[1] // 2
    nf = d_ff // UP_TF
    hosted = _HostedCast(hosted_weight, hosted_rows, 0, lambda i, f: i * nf + f)
    assert hosted.num_blocks <= (m // UP_TM) * nf
    return pl.pallas_call(
        _ffn_up_kernel,
        grid=(m // UP_TM, nf),
        in_specs=[pl.BlockSpec((UP_TM, d), lambda i, f: (i, 0), pipeline_mode=RESIDENT),
                  pl.BlockSpec((d, UP_TF), lambda i, f: (0, f)),
                  pl.BlockSpec((d, UP_TF), lambda i, f: (0, nf + f)),
                  hosted.spec],
        out_specs=[pl.BlockSpec((UP_TM, UP_TF), lambda i, f: (i, f)), hosted.spec],
        out_shape=[jax.ShapeDtypeStruct((m, d_ff), BF16), hosted.out_shape],
        compiler_params=_params(("arbitrary", "arbitrary"), V7X_VMEM_BUDGET),
        name="ffn_up",
    )(h2, w_gate_up, w_gate_up, hosted_weight)


def _ffn_down_kernel(a_ref, w_ref, o_ref):
    for r in range(DOWN_TM // DOWN_RC):
        rows = slice(r * DOWN_RC, (r + 1) * DOWN_RC)
        o_ref[rows, :] = jnp.dot(a_ref[rows, :], w_ref[...], preferred_element_type=F32).astype(BF16)


def _ffn_down(act, w_down):
    m, d_ff = act.shape
    d = w_down.shape[1]
    return pl.pallas_call(
        _ffn_down_kernel,
        grid=(d // DOWN_TN, m // DOWN_TM),
        in_specs=[pl.BlockSpec((DOWN_TM, d_ff), lambda j, i: (i, 0)),
                  pl.BlockSpec((d_ff, DOWN_TN), lambda j, i: (0, j), pipeline_mode=RESIDENT)],
        out_specs=pl.BlockSpec((DOWN_TM, DOWN_TN), lambda j, i: (i, j)),
        out_shape=jax.ShapeDtypeStruct((m, d), BF16),
        compiler_params=_params(("arbitrary", "arbitrary"), V7X_VMEM_BUDGET),
        name="ffn_down",
    )(act, w_down)


def _decay_tables():
    log_g = jnp.log(1.0 - 2.0 ** (-5.0 - jnp.arange(RET_HEADS, dtype=F32)))
    idx = jnp.arange(CHUNK, dtype=F32)
    diff = idx[:, None] - idx[None, :]
    inner = jnp.where((diff >= 0)[None],
                      jnp.exp(log_g[:, None, None] * jnp.maximum(diff, 0.0)[None]), 0.0)
    xi = jnp.exp(log_g[:, None] * (idx[None, :] + 1.0))
    zeta = jnp.exp(log_g[:, None] * (CHUNK - 1.0 - idx[None, :]))
    chunk_decay = jnp.exp(log_g * CHUNK)
    xi_b = jnp.broadcast_to(xi[:, :, None], (RET_HEADS, CHUNK, HEAD_DIM))
    zeta_b = jnp.broadcast_to(zeta[:, :, None], (RET_HEADS, CHUNK, HEAD_DIM))
    return chunk_decay, inner, xi_b, zeta_b


def kernel(x, c, positions, w_ada, b_ada, w_in, sg_ln_g, sg_ln_b, sg_w, sg_b, w_ret_up, w_sg_up,
           w_out, ln1_g, ln1_b, w_gate_up, w_down, ln2_g, ln2_b):
    batch, seq, d = x.shape
    m = batch * seq
    assert w_ada.shape[0] == DEPTH == 1
    x2 = x.reshape(m, d)

    c_pad = jnp.zeros((8, d), F32).at[:batch].set(c)
    b_ada_row = b_ada[0][None, :]
    sh1, sc1 = [t[:, None, :] for t in jnp.split(_ada(c_pad, w_ada[0], b_ada_row, 2 * d)[:batch], 2, axis=-1)]

    h1 = _modulate(x2, sc1, sh1, seq)

    inv_freq = (ROPE_BASE ** (-jnp.arange(0, HEAD_DIM, 2, dtype=F32) / HEAD_DIM))[None, :]
    cos, sin = _rope_tables(positions.astype(F32).reshape(m, 1), inv_freq)

    z, (w_gate_up_bf, w_out_bf, w_ret_up_bf, w_sg_up_bf) = _zproj(
        h1, w_in[0], cos, sin,
        [(w_gate_up[0], 32, 8), (w_out[0], 32, 8), (w_ret_up[0], 16, 40), (w_sg_up[0], 16, 40)])

    chunk_decay, inner_decay, xi_b, zeta_b = _decay_tables()
    ret, sgu = _mixers(z, chunk_decay, inner_decay, xi_b, zeta_b, sg_ln_g[0][None, :],
                       sg_ln_b[0][None, :], sg_w[0], sg_b[0][:, :, None], batch, seq)

    gate_col0 = 4 * RET_HEADS * HEAD_DIM + 2 * SG_GROUPS * SG_GROUP_DIM
    mix = _mix(ret, sgu, w_ret_up_bf, w_sg_up_bf, z, gate_col0)
    branch1, ada_late = _outproj(mix, w_out_bf, c_pad, w_ada[0], b_ada_row, 2 * d)
    g1, sh2, sc2, g2 = [t[:, None, :] for t in jnp.split(ada_late[:batch], 4, axis=-1)]
    ln1_args = [x2, branch1, g1, ln1_g[0][None, :], ln1_b[0][None, :]]
    h2 = _ln_call(_ln1_kernel, "ln1", ln1_args + [sc2, sh2], "ttbrrbb", BF16, seq)

    act, w_down_bf = _ffn_up(h2, w_gate_up_bf, w_down[0], 128)
    ffn = _ffn_down(act, w_down_bf)
    out = _ln_call(_ln2_kernel, "ln2", ln1_args + [ffn, g2, ln2_g[0][None, :], ln2_b[0][None, :]],
                   "ttbrrtbrr", F32, seq)
    return out.reshape(batch, seq, d)
```
